```python
import math, functools
import jax, jax.numpy as jnp
from jax import lax
import numpy as np

D_MODEL = 2048
BATCH = 4
SEQ = 2048
DEPTH = 1
DEC_BATCH = 32
DEC_SEQ = 1
PAST_LEN = 8192
PAGE_SIZE = 128

SSD_EXPAND = 2
SSD_INNER = SSD_EXPAND * D_MODEL
SSD_HEAD_DIM = 64
SSD_HEADS = SSD_INNER // SSD_HEAD_DIM
SSD_GROUPS = 8
SSD_STATE = 128
SSD_CONV = 4
SSD_CHUNK = 128
SSD_CONV_DIM = SSD_INNER + 2 * SSD_GROUPS * SSD_STATE
FOX_HEAD_DIM = 128
FOX_HEADS = D_MODEL // FOX_HEAD_DIM
FOX_WIDTH = FOX_HEADS * FOX_HEAD_DIM
Q_BLOCK = 128
N_MEM = 256
MEM_HEADS = 4
MEM_HEAD_DIM = 128
MEM_WIDTH = MEM_HEADS * MEM_HEAD_DIM
D_FF = 5632
FFN_CONV = 3
N_BRANCH = 3
IN_SPLITS = (SSD_INNER, SSD_CONV_DIM, SSD_HEADS, FOX_WIDTH, FOX_WIDTH, FOX_WIDTH, FOX_HEADS, MEM_WIDTH, N_BRANCH * D_MODEL)
D_IN = sum(IN_SPLITS)
BRANCH_WIDTH = SSD_INNER + FOX_WIDTH + MEM_WIDTH
EPS = 1e-6

kernel_name = 'hybrid_ssd_fox_memory_step'


def rmsnorm(x, g):
    xf = x.astype(jnp.float32)
    y = xf * lax.rsqrt(jnp.mean(xf * xf, axis=-1, keepdims=True) + EPS)
    return (y * g.astype(jnp.float32)).astype(x.dtype)


def causal_dwconv(u, buf, w, b):
    k = w.shape[0]
    l = u.shape[1]
    up = jnp.concatenate([buf.astype(u.dtype), u], axis=1)
    y = up[:, 0:l] * w[0]
    for i in range(1, k):
        y = y + up[:, i:i + l] * w[i]
    return y + b, up[:, up.shape[1] - (k - 1):]


def ssd_scan(xs, dt, a, bm, cm, h0):
    f32 = jnp.float32
    b, l = xs.shape[:2]
    cs = math.gcd(l, SSD_CHUNK)
    nc = l // cs
    r = SSD_HEADS // SSD_GROUPS
    xdt = (xs.astype(f32) * dt[..., None]).reshape(b, nc, cs, SSD_GROUPS, r, SSD_HEAD_DIM)
    da = (dt * a).reshape(b, nc, cs, SSD_GROUPS, r)
    bc = bm.astype(f32).reshape(b, nc, cs, SSD_GROUPS, SSD_STATE)
    cc = cm.astype(f32).reshape(b, nc, cs, SSD_GROUPS, SSD_STATE)
    a_cs = jnp.cumsum(da, axis=2)
    seg = a_cs[:, :, :, None] - a_cs[:, :, None, :]
    causal = jnp.tril(jnp.ones((cs, cs), dtype=bool))[:, :, None, None]
    decay_in = jnp.exp(jnp.where(causal, seg, -jnp.inf))
    cb = jnp.einsum('bclgn,bcsgn->bclsg', cc, bc)
    y_diag = jnp.einsum('bclsg,bclsgr,bcsgrp->bclgrp', cb, decay_in, xdt)
    decay_to_end = jnp.exp(a_cs[:, :, -1:] - a_cs)
    chunk_states = jnp.einsum('bclgn,bclgr,bclgrp->bcgrpn', bc, decay_to_end, xdt)
    chunk_decay = jnp.exp(a_cs[:, :, -1])

    def step(h, inp):
        dec, st = inp
        return h * dec[..., None, None] + st, h

    h_init = h0.astype(f32).reshape(b, SSD_GROUPS, r, SSD_HEAD_DIM, SSD_STATE)
    h_last, h_prev = lax.scan(step, h_init, (jnp.moveaxis(chunk_decay, 1, 0), jnp.moveaxis(chunk_states, 1, 0)))
    h_prev = jnp.moveaxis(h_prev, 0, 1)
    y_off = jnp.einsum('bclgn,bcgrpn,bclgr->bclgrp', cc, h_prev, jnp.exp(a_cs))
    y = (y_diag + y_off).reshape(b, l, SSD_HEADS, SSD_HEAD_DIM)
    return y, h_last.reshape(b, SSD_HEADS, SSD_HEAD_DIM, SSD_STATE)


def ssd_branch(z, xbc, dt_raw, conv_buf, h0, p):
    b, l, _ = xbc.shape
    xbc, new_buf = causal_dwconv(xbc, conv_buf, p['ssd_conv_w'], p['ssd_conv_b'])
    xbc = jax.nn.silu(xbc)
    xs, bm, cm = jnp.split(xbc, [SSD_INNER, SSD_INNER + SSD_GROUPS * SSD_STATE], axis=-1)
    xs = xs.reshape(b, l, SSD_HEADS, SSD_HEAD_DIM)
    bm = bm.reshape(b, l, SSD_GROUPS, SSD_STATE)
    cm = cm.reshape(b, l, SSD_GROUPS, SSD_STATE)
    dt = jax.nn.softplus(dt_raw.astype(jnp.float32) + p['ssd_dt_bias'].astype(jnp.float32))
    a = -jnp.exp(p['ssd_a_log'].astype(jnp.float32))
    y, h_new = ssd_scan(xs, dt, a, bm, cm, h0)
    y = y + xs.astype(jnp.float32) * p['ssd_d'].astype(jnp.float32)[:, None]
    y = y.reshape(b, l, SSD_INNER).astype(z.dtype)
    y = rmsnorm(y * jax.nn.silu(z), p['ssd_g_norm'])
    return y, new_buf, h_new.astype(h0.dtype)


def fox_attend(q, cq, qpos, segments):
    scale = FOX_HEAD_DIM ** -0.5
    cqt = jnp.swapaxes(cq, 1, 2)[..., :, None]
    scores = []
    for k, v, ck, kpos in segments:
        s = jnp.einsum('bqhd,bkhd->bhqk', q, k).astype(jnp.float32) * scale
        s = s + cqt - jnp.swapaxes(ck, 1, 2)[..., None, :]
        s = jnp.where(kpos[None, None, None, :] <= qpos[None, None, :, None], s, -jnp.inf)
        scores.append(s)
    probs = jax.nn.softmax(jnp.concatenate(scores, axis=-1), axis=-1)
    out = None
    off = 0
    for k, v, ck, kpos in segments:
        n = k.shape[1]
        o = jnp.einsum('bhqk,bkhd->bqhd', probs[..., off:off + n].astype(v.dtype), v)
        out = o if out is None else out + o
        off += n
    return out


def fox_prompt(q, k, v, logf):
    b, l = q.shape[:2]
    c = jnp.cumsum(logf, axis=1)
    pos = jnp.arange(l)
    nb = l // Q_BLOCK
    qb = q.reshape(b, nb, Q_BLOCK, FOX_HEADS, FOX_HEAD_DIM).transpose(1, 0, 2, 3, 4)
    cb = c.reshape(b, nb, Q_BLOCK, FOX_HEADS).transpose(1, 0, 2, 3)
    pb = pos.reshape(nb, Q_BLOCK)
    out = lax.map(lambda a: fox_attend(a[0], a[1], a[2], ((k, v, c, pos),)), (qb, cb, pb))
    return out.transpose(1, 0, 2, 3, 4).reshape(b, l, FOX_WIDTH)


def fox_sample(q, k, v, logf, k_pool, v_pool, logf_pool, page_table):
    db, ds = q.shape[:2]
    past = page_table.shape[1] * PAGE_SIZE
    kp = k_pool[page_table].reshape(db, past, FOX_HEADS, FOX_HEAD_DIM)
    vp = v_pool[page_table].reshape(db, past, FOX_HEADS, FOX_HEAD_DIM)
    lfp = logf_pool[page_table].reshape(db, past, FOX_HEADS).astype(jnp.float32)
    c_past = jnp.cumsum(lfp, axis=1)
    c_new = c_past[:, -1:] + jnp.cumsum(logf, axis=1)
    qpos = past + jnp.arange(ds)
    out = fox_attend(q, c_new, qpos, ((kp, vp, c_past, jnp.arange(past)), (k.astype(kp.dtype), v.astype(vp.dtype), c_new, qpos)))
    return out.reshape(db, ds, FOX_WIDTH)


def mem_attend(mq, mk, mv):
    s = jnp.einsum('blhd,bmhd->bhlm', mq, mk).astype(jnp.float32) * (MEM_HEAD_DIM ** -0.5)
    probs = jax.nn.softmax(s, axis=-1).astype(mv.dtype)
    o = jnp.einsum('bhlm,bmhd->blhd', probs, mv)
    return o.reshape(o.shape[0], o.shape[1], MEM_WIDTH)


def layer_forward(x, fox_fn, mem_k, mem_v, ssd_buf, ssd_h0, ffn_buf, p):
    b, l, _ = x.shape
    xn = rmsnorm(x, p['g_mix'])
    proj = xn @ p['w_in']
    points = [int(v) for v in np.cumsum(IN_SPLITS)[:-1]]
    z, xbc, dt_raw, q, k, v, f_raw, mq, gates = jnp.split(proj, points, axis=-1)
    o_ssd, ssd_buf_new, ssd_h_new = ssd_branch(z, xbc, dt_raw, ssd_buf, ssd_h0, p)
    q = q.reshape(b, l, FOX_HEADS, FOX_HEAD_DIM)
    k = k.reshape(b, l, FOX_HEADS, FOX_HEAD_DIM)
    v = v.reshape(b, l, FOX_HEADS, FOX_HEAD_DIM)
    logf = jax.nn.log_sigmoid(f_raw.astype(jnp.float32) + p['fox_b_forget'].astype(jnp.float32))
    o_fox = fox_fn(q, k, v, logf)
    o_mem = mem_attend(mq.reshape(b, l, MEM_HEADS, MEM_HEAD_DIM), mem_k, mem_v)
    w_br = p['w_branch']
    u_ssd = o_ssd @ w_br[:SSD_INNER]
    u_fox = o_fox.astype(x.dtype) @ w_br[SSD_INNER:SSD_INNER + FOX_WIDTH]
    u_mem = o_mem @ w_br[SSD_INNER + FOX_WIDTH:]
    g = jax.nn.sigmoid(gates).reshape(b, l, N_BRANCH, D_MODEL)
    merged = g[:, :, 0] * u_ssd + g[:, :, 1] * u_fox + g[:, :, 2] * u_mem
    h = x + merged @ p['w_out']
    hn = rmsnorm(h, p['g_ffn'])
    gate_c, ffn_buf_new = causal_dwconv(hn @ p['w_ffn_gate'], ffn_buf, p['ffn_conv_w'], p['ffn_conv_b'])
    h = h + (jax.nn.silu(gate_c) * (hn @ p['w_ffn_up'])) @ p['w_ffn_down']
    return h, k, v, logf, ssd_buf_new, ssd_h_new, ffn_buf_new


def setup_inputs(seed: int = 0) -> dict:
    key = jax.random.key(seed)
    ks = iter(jax.random.split(key, 48))
    f32 = jnp.float32

    def nrm(shape, scale):
        return jax.random.normal(next(ks), shape, f32) * scale

    L = DEPTH
    n_pages = PAST_LEN // PAGE_SIZE
    n_pool = (DEC_BATCH * n_pages * 5) // 4
    page_table = jax.random.permutation(next(ks), n_pool)[: DEC_BATCH * n_pages].reshape(DEC_BATCH, n_pages).astype(jnp.int32)
    dt0 = jnp.exp(jax.random.uniform(next(ks), (L, SSD_HEADS), f32, minval=math.log(1e-3), maxval=math.log(1e-1)))
    dt_bias = dt0 + jnp.log(-jnp.expm1(-dt0))
    a_log = jnp.log(jax.random.uniform(next(ks), (L, SSD_HEADS), f32, minval=1.0, maxval=16.0))
    w_branch = jnp.concatenate([nrm((L, SSD_INNER, D_MODEL), SSD_INNER ** -0.5),
                                nrm((L, FOX_WIDTH, D_MODEL), FOX_WIDTH ** -0.5),
                                nrm((L, MEM_WIDTH, D_MODEL), MEM_WIDTH ** -0.5)], axis=1)
    return {
        'x_prompt': nrm((BATCH, SEQ, D_MODEL), 1.0),
        'x_sample': nrm((DEC_BATCH, DEC_SEQ, D_MODEL), 1.0),
        'cache_fox_k': nrm((L, n_pool, PAGE_SIZE, FOX_HEADS, FOX_HEAD_DIM), 1.0),
        'cache_fox_v': nrm((L, n_pool, PAGE_SIZE, FOX_HEADS, FOX_HEAD_DIM), 1.0),
        'cache_fox_logf': jax.nn.log_sigmoid(nrm((L, n_pool, PAGE_SIZE, FOX_HEADS), 0.5) + 3.0),
        'cache_mem_k': nrm((L, DEC_BATCH, N_MEM, MEM_HEADS, MEM_HEAD_DIM), 1.0),
        'cache_mem_v': nrm((L, DEC_BATCH, N_MEM, MEM_HEADS, MEM_HEAD_DIM), 1.0),
        'state_ssd': nrm((L, DEC_BATCH, SSD_HEADS, SSD_HEAD_DIM, SSD_STATE), 0.5),
        'state_ssd_conv': nrm((L, DEC_BATCH, SSD_CONV - 1, SSD_CONV_DIM), 1.0),
        'state_ffn_conv': nrm((L, DEC_BATCH, FFN_CONV - 1, D_FF), 1.0),
        'page_table': page_table,
        'mem_prompt': nrm((BATCH, N_MEM, D_MODEL), 1.0),
        'g_mix': 1.0 + nrm((L, D_MODEL), 0.02),
        'w_in': nrm((L, D_MODEL, D_IN), D_MODEL ** -0.5),
        'ssd_conv_w': nrm((L, SSD_CONV, SSD_CONV_DIM), SSD_CONV ** -0.5),
        'ssd_conv_b': nrm((L, SSD_CONV_DIM), 0.02),
        'ssd_dt_bias': dt_bias,
        'ssd_a_log': a_log,
        'ssd_d': 1.0 + nrm((L, SSD_HEADS), 0.02),
        'ssd_g_norm': 1.0 + nrm((L, SSD_INNER), 0.02),
        'fox_b_forget': 3.0 + nrm((L, FOX_HEADS), 0.5),
        'g_mem': 1.0 + nrm((L, D_MODEL), 0.02),
        'w_mem_kv': nrm((L, D_MODEL, 2 * MEM_WIDTH), D_MODEL ** -0.5),
        'w_branch': w_branch,
        'w_out': nrm((L, D_MODEL, D_MODEL), D_MODEL ** -0.5),
        'g_ffn': 1.0 + nrm((L, D_MODEL), 0.02),
        'w_ffn_gate': nrm((L, D_MODEL, D_FF), D_MODEL ** -0.5),
        'w_ffn_up': nrm((L, D_MODEL, D_FF), D_MODEL ** -0.5),
        'ffn_conv_w': nrm((L, FFN_CONV, D_FF), FFN_CONV ** -0.5),
        'ffn_conv_b': nrm((L, D_FF), 0.02),
        'w_ffn_down': nrm((L, D_FF, D_MODEL), D_FF ** -0.5),
        'g_final': 1.0 + nrm((D_MODEL,), 0.02),
    }


def reference(x_prompt, x_sample, cache_fox_k, cache_fox_v, cache_fox_logf, cache_mem_k, cache_mem_v,
              state_ssd, state_ssd_conv, state_ffn_conv, page_table, mem_prompt,
              g_mix, w_in, ssd_conv_w, ssd_conv_b, ssd_dt_bias, ssd_a_log, ssd_d, ssd_g_norm,
              fox_b_forget, g_mem, w_mem_kv, w_branch, w_out, g_ffn, w_ffn_gate, w_ffn_up,
              ffn_conv_w, ffn_conv_b, w_ffn_down, g_final):
    bp = x_prompt.shape[0]
    hp = x_prompt
    hs = x_sample
    kp_l, vp_l, lfp_l, mkp_l, mvp_l, shp_l, scp_l, fcp_l = [], [], [], [], [], [], [], []
    ks_l, vs_l, lfs_l, shs_l, scs_l, fcs_l = [], [], [], [], [], []
    for l in range(DEPTH):
        p = {'g_mix': g_mix[l], 'w_in': w_in[l], 'ssd_conv_w': ssd_conv_w[l], 'ssd_conv_b': ssd_conv_b[l],
             'ssd_dt_bias': ssd_dt_bias[l], 'ssd_a_log': ssd_a_log[l], 'ssd_d': ssd_d[l], 'ssd_g_norm': ssd_g_norm[l],
             'fox_b_forget': fox_b_forget[l], 'w_branch': w_branch[l], 'w_out': w_out[l], 'g_ffn': g_ffn[l],
             'w_ffn_gate': w_ffn_gate[l], 'w_ffn_up': w_ffn_up[l], 'ffn_conv_w': ffn_conv_w[l],
             'ffn_conv_b': ffn_conv_b[l], 'w_ffn_down': w_ffn_down[l]}
        mkv = rmsnorm(mem_prompt, g_mem[l]) @ w_mem_kv[l]
        mk_p = mkv[..., :MEM_WIDTH].reshape(bp, N_MEM, MEM_HEADS, MEM_HEAD_DIM)
        mv_p = mkv[..., MEM_WIDTH:].reshape(bp, N_MEM, MEM_HEADS, MEM_HEAD_DIM)
        zero_sbuf = jnp.zeros((bp, SSD_CONV - 1, SSD_CONV_DIM), x_prompt.dtype)
        zero_h = jnp.zeros((bp, SSD_HEADS, SSD_HEAD_DIM, SSD_STATE), x_prompt.dtype)
        zero_fbuf = jnp.zeros((bp, FFN_CONV - 1, D_FF), x_prompt.dtype)
        hp, k_p, v_p, lf_p, sc_p, sh_p, fc_p = layer_forward(hp, fox_prompt, mk_p, mv_p, zero_sbuf, zero_h, zero_fbuf, p)
        fox_fn_s = functools.partial(fox_sample, k_pool=cache_fox_k[l], v_pool=cache_fox_v[l],
                                     logf_pool=cache_fox_logf[l], page_table=page_table)
        hs, k_s, v_s, lf_s, sc_s, sh_s, fc_s = layer_forward(hs, fox_fn_s, cache_mem_k[l], cache_mem_v[l],
                                                             state_ssd_conv[l], state_ssd[l], state_ffn_conv[l], p)
        kp_l.append(k_p); vp_l.append(v_p); lfp_l.append(lf_p); mkp_l.append(mk_p); mvp_l.append(mv_p)
        shp_l.append(sh_p); scp_l.append(sc_p); fcp_l.append(fc_p)
        ks_l.append(k_s); vs_l.append(v_s); lfs_l.append(lf_s); shs_l.append(sh_s); scs_l.append(sc_s); fcs_l.append(fc_s)
    y_prompt = rmsnorm(hp, g_final)
    y_sample = rmsnorm(hs, g_final)
    fox_k_p = jnp.stack(kp_l)
    fox_v_p = jnp.stack(vp_l)
    fox_logf_p = jnp.stack(lfp_l)
    mem_k_p = jnp.stack(mkp_l)
    mem_v_p = jnp.stack(mvp_l)
    ssd_state_p = jnp.stack(shp_l)
    ssd_conv_p = jnp.stack(scp_l)
    ffn_conv_p = jnp.stack(fcp_l)
    fox_k_s = jnp.stack(ks_l)
    fox_v_s = jnp.stack(vs_l)
    fox_logf_s = jnp.stack(lfs_l)
    ssd_state_s = jnp.stack(shs_l)
    ssd_conv_s = jnp.stack(scs_l)
    ffn_conv_s = jnp.stack(fcs_l)
    return (y_prompt, y_sample, fox_k_p, fox_v_p, fox_logf_p, mem_k_p, mem_v_p, ssd_state_p, ssd_conv_p, ffn_conv_p,
            fox_k_s, fox_v_s, fox_logf_s, ssd_state_s, ssd_conv_s, ffn_conv_s)
```

```python
import functools
import math

import jax
import jax.numpy as jnp
from jax import lax
from jax.experimental import pallas as pl
from jax.experimental.pallas import tpu as pltpu

F32 = jnp.float32
BF16 = jnp.bfloat16

D_MODEL = 2048
SSD_INNER = 4096
SSD_HEAD_DIM = 64
SSD_HEADS = 64
SSD_GROUPS = 8
SSD_STATE = 128
SSD_CONV = 4
SSD_CHUNK = 128
SSD_CONV_DIM = SSD_INNER + 2 * SSD_GROUPS * SSD_STATE
HEADS_PER_GROUP = SSD_HEADS // SSD_GROUPS
GROUP_WIDTH = HEADS_PER_GROUP * SSD_HEAD_DIM
FOX_HEAD_DIM = 128
FOX_HEADS = 16
FOX_WIDTH = FOX_HEADS * FOX_HEAD_DIM
MEM_HEADS = 4
MEM_HEAD_DIM = 128
MEM_WIDTH = MEM_HEADS * MEM_HEAD_DIM
D_FF = 5632
FFN_CONV = 3
N_BRANCH = 3
EPS = 1e-6

LANES = 128
SUBLANES = 8
VMEM_LIMIT_BYTES = 56 * 1024 * 1024

DTF_WIDTH = LANES
PAGES_PER_STEP = 4


def _cparams(*semantics):
    return pltpu.CompilerParams(dimension_semantics=semantics, vmem_limit_bytes=VMEM_LIMIT_BYTES)


def _dot(a, b):
    return jnp.dot(a, b, preferred_element_type=F32)


def _dot_nt(a, b):
    return lax.dot_general(a, b, (((1,), (1,)), ((), ())), preferred_element_type=F32)


def _silu(x):
    return x * jax.nn.sigmoid(x)


def _softplus(x):
    return jnp.maximum(x, 0.0) + jnp.log1p(jnp.exp(-jnp.abs(x)))


def _split3(x):
    hi = x.astype(BF16)
    r1 = x - hi.astype(F32)
    mid = r1.astype(BF16)
    lo = (r1 - mid.astype(F32)).astype(BF16)
    return hi, mid, lo


def _rms_kernel(x_ref, g_ref, o_ref):
    x = x_ref[...]
    ms = jnp.mean(x * x, axis=-1, keepdims=True)
    o_ref[...] = (x * lax.rsqrt(ms + EPS) * g_ref[...]).astype(o_ref.dtype)


def rms_norm(x, g, *, tm, out_dtype=BF16):
    m, d = x.shape
    return pl.pallas_call(
        _rms_kernel,
        grid=(m // tm,),
        in_specs=[pl.BlockSpec((tm, d), lambda i: (i, 0)), pl.BlockSpec((1, d), lambda i: (0, 0))],
        out_specs=pl.BlockSpec((tm, d), lambda i: (i, 0)),
        out_shape=jax.ShapeDtypeStruct((m, d), out_dtype),
        compiler_params=_cparams("parallel"),
        name="rms_norm",
    )(x, g.reshape(1, d))


def _mm_kernel(*refs, epilogue, n_rows):
    a_ref, w_ref = refs[0], refs[1]
    rows = [r[...] for r in refs[2:2 + n_rows]]
    outs = refs[2 + n_rows:]
    acc = _dot(a_ref[...], w_ref[...])
    vals = epilogue(acc, *rows)
    for o, v in zip(outs, vals):
        o[...] = v.astype(o.dtype)


def matmul(a, w, *, tm, tn, out_dtypes, epilogue=lambda acc: (acc,), rows=(), name="matmul"):
    m, k = a.shape
    n = w.shape[1]
    tm = min(tm, m)
    tn = min(tn, n)
    in_specs = [pl.BlockSpec((tm, k), lambda i, j: (i, 0)), pl.BlockSpec((k, tn), lambda i, j: (0, j))]
    in_specs += [pl.BlockSpec((1, tn), lambda i, j: (0, j)) for _ in rows]
    outs = pl.pallas_call(
        functools.partial(_mm_kernel, epilogue=epilogue, n_rows=len(rows)),
        grid=(m // tm, n // tn),
        in_specs=in_specs,
        out_specs=[pl.BlockSpec((tm, tn), lambda i, j: (i, j)) for _ in out_dtypes],
        out_shape=[jax.ShapeDtypeStruct((m, n), dt) for dt in out_dtypes],
        compiler_params=_cparams("parallel", "arbitrary"),
        name=name,
    )(a, w, *rows)
    return outs


def _dtf_epilogue(acc, bias):
    x = acc + bias
    lane = lax.broadcasted_iota(jnp.int32, x.shape, 1)
    return (jnp.where(lane < SSD_HEADS, _softplus(x), -_softplus(-x)),)


def _cumsum_kernel(x_ref, o_ref):
    x = x_ref[...]
    n = x.shape[0]
    row = lax.broadcasted_iota(jnp.int32, x.shape, 0)
    s = 1
    while s < n:
        x = x + jnp.where(row >= s, pltpu.roll(x, s, 0), 0.0)
        s *= 2
    o_ref[...] = x


def seq_cumsum(x, *, seq):
    m, d = x.shape
    return pl.pallas_call(
        _cumsum_kernel,
        grid=(m // seq,),
        in_specs=[pl.BlockSpec((seq, d), lambda b: (b, 0))],
        out_specs=pl.BlockSpec((seq, d), lambda b: (b, 0)),
        out_shape=jax.ShapeDtypeStruct((m, d), F32),
        compiler_params=_cparams("parallel"),
        name="seq_cumsum",
    )(x)


def _pair_cols(mat, pair, lane_lo):
    a = mat[:, 2 * pair:2 * pair + 1]
    b = mat[:, 2 * pair + 1:2 * pair + 2]
    return jnp.where(lane_lo, a, b)


def _ssd_kernel(xbc_ref, dtf_ref, zs_ref, cw_ref, cb_ref, a_ref, dskip_ref, gn_ref,
                o_ref, st_out_ref, xbuf, xc, st, y_scr, xw_scr):
    c = pl.program_id(1)
    nc = pl.num_programs(1)
    cs = SSD_CHUNK
    halo = SUBLANES

    @pl.when(c == 0)
    def _():
        xbuf[0:halo, :] = jnp.zeros((halo, SSD_CONV_DIM), F32)
        st[...] = jnp.zeros_like(st)

    @pl.when(c > 0)
    def _():
        xbuf[0:halo, :] = xbuf[cs:cs + halo, :]

    xbuf[halo:halo + cs, :] = xbc_ref[...]

    cw = cw_ref[...]
    cb = cb_ref[...]
    col_chunk = 512
    for j in range(SSD_CONV_DIM // col_chunk):
        sl = slice(j * col_chunk, (j + 1) * col_chunk)
        acc = xbuf[halo - 3:halo - 3 + cs, sl] * cw[0:1, sl]
        for i in range(1, SSD_CONV):
            acc = acc + xbuf[halo - 3 + i:halo - 3 + i + cs, sl] * cw[i:i + 1, sl]
        xc[:, sl] = _silu(acc + cb[:, sl])

    dt = dtf_ref[...]
    da = dt * a_ref[...]
    r_i = lax.broadcasted_iota(jnp.int32, (cs, cs), 0)
    c_i = lax.broadcasted_iota(jnp.int32, (cs, cs), 1)
    causal = c_i <= r_i
    tril = jnp.where(causal, 1.0, 0.0)
    a_cs = jnp.dot(tril, da, preferred_element_type=F32, precision=lax.Precision.HIGHEST)
    a_cs_t = a_cs.T
    dt_t = dt.T
    last = a_cs[cs - 1:cs, :]
    ecs = jnp.exp(a_cs)
    wfac = dt * jnp.exp(last - a_cs)
    cdec = jnp.exp(last)

    lane_lo = lax.broadcasted_iota(jnp.int32, (cs, LANES), 1) < SSD_HEAD_DIM
    lane_lo_row = lane_lo[0:1, :]
    pairs_per_group = HEADS_PER_GROUP // 2
    for g in range(SSD_GROUPS):
        b_off = SSD_INNER + g * SSD_STATE
        c_off = SSD_INNER + SSD_GROUPS * SSD_STATE + g * SSD_STATE
        bg = xc[:, b_off:b_off + SSD_STATE]
        cg = xc[:, c_off:c_off + SSD_STATE].astype(BF16)
        cbm = _dot_nt(cg, bg.astype(BF16))
        yoff = _dot(cg, st[g].astype(BF16))
        cd_tiles = []
        for pr in range(pairs_per_group):
            pair = g * pairs_per_group + pr
            col0 = pair * LANES
            xs_pair = xc[:, col0:col0 + LANES]
            xs_bf = xs_pair.astype(BF16)
            yd = []
            for h in (2 * pair, 2 * pair + 1):
                seg = a_cs[:, h:h + 1] - a_cs_t[h:h + 1, :]
                decay = jnp.exp(jnp.where(causal, seg, -jnp.inf))
                mh = (cbm * decay * dt_t[h:h + 1, :]).astype(BF16)
                yd.append(_dot(mh, xs_bf))
            y_pair = jnp.where(lane_lo, yd[0], yd[1])
            y_pair = y_pair + _pair_cols(ecs, pair, lane_lo) * yoff[:, pr * LANES:(pr + 1) * LANES]
            y_pair = y_pair + xs_pair * dskip_ref[:, col0:col0 + LANES]
            y_scr[:, col0:col0 + LANES] = y_pair
            xw_scr[:, pr * LANES:(pr + 1) * LANES] = (xs_pair * _pair_cols(wfac, pair, lane_lo)).astype(BF16)
            cd_tiles.append(_pair_cols(cdec, pair, lane_lo_row))
        cd_row = jnp.concatenate(cd_tiles, axis=1)
        st[g] = st[g] * cd_row + _dot(bg.T.astype(BF16), xw_scr[...])

    v = y_scr[...] * zs_ref[...]
    ms = jnp.mean(v * v, axis=-1, keepdims=True)
    o_ref[...] = (v * lax.rsqrt(ms + EPS) * gn_ref[...]).astype(o_ref.dtype)

    @pl.when(c == nc - 1)
    def _():
        for g in range(SSD_GROUPS):
            st_out_ref[0, g * GROUP_WIDTH:(g + 1) * GROUP_WIDTH, :] = st[g].T


def ssd_prompt(xbc, dtf, zs, conv_w, conv_b, a_row, dskip_row, gnorm, *, batch, seq):
    nc = seq // SSD_CHUNK
    cs = SSD_CHUNK
    row = lambda b, c: (b * nc + c, 0)
    const = lambda b, c: (0, 0)
    return pl.pallas_call(
        _ssd_kernel,
        grid=(batch, nc),
        in_specs=[
            pl.BlockSpec((cs, SSD_CONV_DIM), row),
            pl.BlockSpec((cs, DTF_WIDTH), row),
            pl.BlockSpec((cs, SSD_INNER), row),
            pl.BlockSpec((SSD_CONV, SSD_CONV_DIM), const),
            pl.BlockSpec((1, SSD_CONV_DIM), const),
            pl.BlockSpec((1, DTF_WIDTH), const),
            pl.BlockSpec((1, SSD_INNER), const),
            pl.BlockSpec((1, SSD_INNER), const),
        ],
        out_specs=[
            pl.BlockSpec((cs, SSD_INNER), row),
            pl.BlockSpec((1, SSD_HEADS * SSD_HEAD_DIM, SSD_STATE), lambda b, c: (b, 0, 0)),
        ],
        out_shape=[
            jax.ShapeDtypeStruct((batch * seq, SSD_INNER), BF16),
            jax.ShapeDtypeStruct((batch, SSD_HEADS * SSD_HEAD_DIM, SSD_STATE), F32),
        ],
        scratch_shapes=[
            pltpu.VMEM((cs + SUBLANES, SSD_CONV_DIM), F32),
            pltpu.VMEM((cs, SSD_CONV_DIM), F32),
            pltpu.VMEM((SSD_GROUPS, SSD_STATE, GROUP_WIDTH), F32),
            pltpu.VMEM((cs, SSD_INNER), F32),
            pltpu.VMEM((cs, GROUP_WIDTH), BF16),
        ],
        compiler_params=_cparams("parallel", "arbitrary"),
        name="ssd_prompt",
    )(xbc, dtf, zs, conv_w, conv_b, a_row, dskip_row, gnorm)


def _fox_kernel(q_ref, k_ref, v_ref, cq_ref, ck_ref, o_ref, *, blk, scale):
    qi = pl.program_id(2)
    q = q_ref[...]
    cq = cq_ref[0, 0]

    def step(j, carry, masked):
        m, l, acc = carry
        off = pl.multiple_of(j * blk, blk)
        k = k_ref[pl.ds(off, blk), :]
        v = v_ref[pl.ds(off, blk), :]
        s = _dot_nt(q, k) * scale
        s = s + cq - ck_ref[0, 0, j]
        if masked:
            r_i = lax.broadcasted_iota(jnp.int32, s.shape, 0)
            c_i = lax.broadcasted_iota(jnp.int32, s.shape, 1)
            s = jnp.where(c_i <= r_i, s, -jnp.inf)
        m_new = jnp.maximum(m, jnp.max(s, axis=-1, keepdims=True))
        p = jnp.exp(s - m_new)
        alpha = jnp.exp(m - m_new)
        l = alpha * l + jnp.sum(p, axis=-1, keepdims=True)
        acc = alpha * acc + _dot(p.astype(BF16), v)
        return m_new, l, acc

    init = (jnp.full((blk, 1), -jnp.inf, F32), jnp.zeros((blk, 1), F32), jnp.zeros((blk, FOX_HEAD_DIM), F32))
    carry = lax.fori_loop(0, qi, lambda j, cr: step(j, cr, False), init)
    _, l, acc = step(qi, carry, True)
    o_ref[...] = (acc / l).astype(o_ref.dtype)


def fox_prompt(q, k, v, cq, ck, *, batch, seq, blk):
    nq = seq // blk
    return pl.pallas_call(
        functools.partial(_fox_kernel, blk=blk, scale=FOX_HEAD_DIM ** -0.5),
        grid=(batch, FOX_HEADS, nq),
        in_specs=[
            pl.BlockSpec((blk, FOX_HEAD_DIM), lambda b, h, i: (b * nq + i, h)),
            pl.BlockSpec((seq, FOX_HEAD_DIM), lambda b, h, i: (b, h)),
            pl.BlockSpec((seq, FOX_HEAD_DIM), lambda b, h, i: (b, h)),
            pl.BlockSpec((1, 1, blk, 1), lambda b, h, i: (b, h, i, 0)),
            pl.BlockSpec((1, 1, nq, 1, blk), lambda b, h, i: (b, h, 0, 0, 0)),
        ],
        out_specs=pl.BlockSpec((blk, FOX_HEAD_DIM), lambda b, h, i: (b * nq + i, h)),
        out_shape=jax.ShapeDtypeStruct((batch * seq, FOX_WIDTH), BF16),
        compiler_params=_cparams("parallel", "parallel", "arbitrary"),
        name="fox_prompt",
    )(q, k, v, cq, ck)


def _mem_kernel(q_ref, k_ref, v_ref, o_ref, *, scale):
    s = _dot_nt(q_ref[...], k_ref[...]) * scale
    e = jnp.exp(s - jnp.max(s, axis=-1, keepdims=True))
    p = e / jnp.sum(e, axis=-1, keepdims=True)
    o_ref[...] = _dot(p.astype(BF16), v_ref[...]).astype(o_ref.dtype)


def mem_prompt_attend(q, k, v, *, batch, seq, n_mem, tq):
    nq = seq // tq
    return pl.pallas_call(
        functools.partial(_mem_kernel, scale=MEM_HEAD_DIM ** -0.5),
        grid=(batch, MEM_HEADS, nq),
        in_specs=[
            pl.BlockSpec((tq, MEM_HEAD_DIM), lambda b, h, i: (b * nq + i, h)),
            pl.BlockSpec((n_mem, MEM_HEAD_DIM), lambda b, h, i: (b, h)),
            pl.BlockSpec((n_mem, MEM_HEAD_DIM), lambda b, h, i: (b, h)),
        ],
        out_specs=pl.BlockSpec((tq, MEM_HEAD_DIM), lambda b, h, i: (b * nq + i, h)),
        out_shape=jax.ShapeDtypeStruct((batch * seq, MEM_WIDTH), BF16),
        compiler_params=_cparams("parallel", "parallel", "arbitrary"),
        name="mem_prompt_attend",
    )(q, k, v)


def _merge_kernel(os_ref, of_ref, om_ref, ws_ref, wf_ref, wm_ref, g0_ref, g1_ref, g2_ref, o_ref):
    u = g0_ref[...] * _dot(os_ref[...], ws_ref[...])
    u = u + g1_ref[...] * _dot(of_ref[...], wf_ref[...])
    u = u + g2_ref[...] * _dot(om_ref[...], wm_ref[...])
    o_ref[...] = u.astype(o_ref.dtype)


def branch_merge(o_ssd, o_fox, o_mem, w_s, w_f, w_m, gates, *, tm, tn):
    m = o_ssd.shape[0]
    tm = min(tm, m)
    nj = D_MODEL // tn
    act = lambda k: pl.BlockSpec((tm, k), lambda i, j: (i, 0))
    wgt = lambda k: pl.BlockSpec((k, tn), lambda i, j: (0, j))
    gate = lambda br: pl.BlockSpec((tm, tn), lambda i, j: (i, br * nj + j))
    return pl.pallas_call(
        _merge_kernel,
        grid=(m // tm, nj),
        in_specs=[act(SSD_INNER), act(FOX_WIDTH), act(MEM_WIDTH), wgt(SSD_INNER), wgt(FOX_WIDTH), wgt(MEM_WIDTH),
                  gate(0), gate(1), gate(2)],
        out_specs=pl.BlockSpec((tm, tn), lambda i, j: (i, j)),
        out_shape=jax.ShapeDtypeStruct((m, D_MODEL), BF16),
        compiler_params=_cparams("parallel", "arbitrary"),
        name="branch_merge",
    )(o_ssd, o_fox, o_mem, w_s, w_f, w_m, gates, gates, gates)


def _outproj_kernel(a_ref, w_ref, x_ref, g_ref, h_ref, hn_ref):
    h = x_ref[...] + _dot(a_ref[...], w_ref[...])
    h_ref[...] = h
    ms = jnp.mean(h * h, axis=-1, keepdims=True)
    hn_ref[...] = (h * lax.rsqrt(ms + EPS) * g_ref[...]).astype(hn_ref.dtype)


def out_proj(merged, w_out, x, g_ffn, *, tm):
    m = x.shape[0]
    tm = min(tm, m)
    row = lambda i: (i, 0)
    const = lambda i: (0, 0)
    return pl.pallas_call(
        _outproj_kernel,
        grid=(m // tm,),
        in_specs=[pl.BlockSpec((tm, D_MODEL), row), pl.BlockSpec((D_MODEL, D_MODEL), const),
                  pl.BlockSpec((tm, D_MODEL), row), pl.BlockSpec((1, D_MODEL), const)],
        out_specs=[pl.BlockSpec((tm, D_MODEL), row), pl.BlockSpec((tm, D_MODEL), row)],
        out_shape=[jax.ShapeDtypeStruct((m, D_MODEL), F32), jax.ShapeDtypeStruct((m, D_MODEL), BF16)],
        compiler_params=_cparams("parallel"),
        name="out_proj",
    )(merged, w_out, x, g_ffn.reshape(1, D_MODEL))


def _ffn_kernel(hn_ref, wg_ref, wu_ref, cw_ref, cb_ref, act_ref, buf_ref, gbuf, *, tiles_per_seq):
    i = pl.program_id(1)
    tm = hn_ref.shape[0]
    halo = SUBLANES

    @pl.when(i % tiles_per_seq == 0)
    def _():
        gbuf[0:halo, :] = jnp.zeros((halo, gbuf.shape[1]), F32)

    @pl.when(i % tiles_per_seq != 0)
    def _():
        gbuf[0:halo, :] = gbuf[tm:tm + halo, :]

    hn = hn_ref[...]
    gbuf[halo:halo + tm, :] = _dot(hn, wg_ref[...])
    cw = cw_ref[...]
    conv = gbuf[halo - 2:halo - 2 + tm, :] * cw[0:1, :]
    conv = conv + gbuf[halo - 1:halo - 1 + tm, :] * cw[1:2, :]
    conv = conv + gbuf[halo:halo + tm, :] * cw[2:3, :]
    conv = conv + cb_ref[...]
    act_ref[...] = (_silu(conv) * _dot(hn, wu_ref[...])).astype(act_ref.dtype)
    buf_ref[0] = gbuf[halo + tm - 2:halo + tm, :]


def ffn_gate_up(hn, w_gate, w_up, conv_w, conv_b, *, batch, seq, tm, tn):
    m = hn.shape[0]
    tiles_per_seq = seq // tm
    return pl.pallas_call(
        functools.partial(_ffn_kernel, tiles_per_seq=tiles_per_seq),
        grid=(D_FF // tn, m // tm),
        in_specs=[
            pl.BlockSpec((tm, D_MODEL), lambda j, i: (i, 0)),
            pl.BlockSpec((D_MODEL, tn), lambda j, i: (0, j)),
            pl.BlockSpec((D_MODEL, tn), lambda j, i: (0, j)),
            pl.BlockSpec((FFN_CONV, tn), lambda j, i: (0, j)),
            pl.BlockSpec((1, tn), lambda j, i: (0, j)),
        ],
        out_specs=[
            pl.BlockSpec((tm, tn), lambda j, i: (i, j)),
            pl.BlockSpec((1, FFN_CONV - 1, tn), lambda j, i: (i // tiles_per_seq, 0, j)),
        ],
        out_shape=[
            jax.ShapeDtypeStruct((m, D_FF), BF16),
            jax.ShapeDtypeStruct((batch, FFN_CONV - 1, D_FF), F32),
        ],
        scratch_shapes=[pltpu.VMEM((tm + SUBLANES, tn), F32)],
        compiler_params=_cparams("parallel", "arbitrary"),
        name="ffn_gate_up",
    )(hn, w_gate, w_up, conv_w, conv_b)


def _down_kernel(a_ref, w_ref, h_ref, g_ref, y_ref, acc):
    k = pl.program_id(1)

    @pl.when(k == 0)
    def _():
        acc[...] = h_ref[...]

    acc[...] += _dot(a_ref[...], w_ref[...])

    @pl.when(k == pl.num_programs(1) - 1)
    def _():
        h = acc[...]
        ms = jnp.mean(h * h, axis=-1, keepdims=True)
        y_ref[...] = h * lax.rsqrt(ms + EPS) * g_ref[...]


def ffn_down(act, w_down, h, g_final, *, tm, tk):
    m = h.shape[0]
    tm = min(tm, m)
    return pl.pallas_call(
        _down_kernel,
        grid=(m // tm, D_FF // tk),
        in_specs=[pl.BlockSpec((tm, tk), lambda i, k: (i, k)), pl.BlockSpec((tk, D_MODEL), lambda i, k: (k, 0)),
                  pl.BlockSpec((tm, D_MODEL), lambda i, k: (i, 0)), pl.BlockSpec((1, D_MODEL), lambda i, k: (0, 0))],
        out_specs=pl.BlockSpec((tm, D_MODEL), lambda i, k: (i, 0)),
        out_shape=jax.ShapeDtypeStruct((m, D_MODEL), F32),
        scratch_shapes=[pltpu.VMEM((tm, D_MODEL), F32)],
        compiler_params=_cparams("parallel", "arbitrary"),
        name="ffn_down",
    )(act, w_down, h, g_final.reshape(1, D_MODEL))


def _ssd_s_prep_kernel(xbc_ref, buf_ref, cw_ref, cb_ref, dtf_ref, a_ref, expand_ref, xc_ref, cols_ref):
    cw = cw_ref[...]
    acc = buf_ref[0] * cw[0:1, :]
    acc = acc + buf_ref[1] * cw[1:2, :]
    acc = acc + buf_ref[2] * cw[2:3, :]
    acc = acc + xbc_ref[...] * cw[3:4, :]
    xcv = _silu(acc + cb_ref[...])
    xc_ref[...] = xcv
    dt = dtf_ref[...]
    decay = jnp.exp(dt * a_ref[...])
    nb = dt.shape[0]
    parts = _split3(dt) + _split3(decay)
    stacked = jnp.concatenate(parts, axis=0)
    ex = _dot(stacked, expand_ref[...])
    dt_exp = ex[0:nb] + ex[nb:2 * nb] + ex[2 * nb:3 * nb]
    dec_exp = ex[3 * nb:4 * nb] + ex[4 * nb:5 * nb] + ex[5 * nb:6 * nb]
    xdt = xcv[:, 0:SSD_INNER] * dt_exp
    pad = jnp.zeros((LANES - 2 * nb, SSD_INNER), F32)
    cols_ref[...] = jnp.concatenate([xdt, dec_exp, pad], axis=0).T


def ssd_sample_prep(xbc, buf_t, conv_w, conv_b, dtf, a_row, expand):
    nb = xbc.shape[0]
    full = lambda s: pl.BlockSpec(s, lambda i: (0,) * len(s))
    return pl.pallas_call(
        _ssd_s_prep_kernel,
        grid=(1,),
        in_specs=[full(xbc.shape), full(buf_t.shape), full(conv_w.shape), full(conv_b.shape), full(dtf.shape),
                  full(a_row.shape), full(expand.shape)],
        out_specs=[full((nb, SSD_CONV_DIM)), full((SSD_INNER, LANES))],
        out_shape=[jax.ShapeDtypeStruct((nb, SSD_CONV_DIM), F32), jax.ShapeDtypeStruct((SSD_INNER, LANES), F32)],
        compiler_params=_cparams("arbitrary"),
        name="ssd_sample_prep",
    )(xbc, buf_t, conv_w, conv_b, dtf, a_row, expand)


def _ssd_s_step_kernel(cols_ref, h0_ref, xc_ref, zs_ref, dskip_ref, gn_ref, hn_ref, o_ref, *, nb):
    b = pl.program_id(0)
    cols = cols_ref[...]
    hi, mid, lo = _split3(cols)
    lhs = jnp.concatenate([hi, mid, lo], axis=1)
    r_i = lax.broadcasted_iota(jnp.int32, (3 * LANES, 2 * LANES), 0) % LANES
    c_i = lax.broadcasted_iota(jnp.int32, (3 * LANES, 2 * LANES), 1)
    sel = jnp.where(r_i == jnp.where(c_i < LANES, b, nb + b), 1.0, 0.0).astype(BF16)
    picked = _dot(lhs, sel)
    xb = picked[:, 0:LANES]
    db = picked[:, LANES:2 * LANES]
    xc = xc_ref[0]
    y_parts = []
    for g in range(SSD_GROUPS):
        rows = slice(g * GROUP_WIDTH, (g + 1) * GROUP_WIDTH)
        b_row = xc[:, SSD_INNER + g * SSD_STATE:SSD_INNER + (g + 1) * SSD_STATE]
        c_off = SSD_INNER + SSD_GROUPS * SSD_STATE + g * SSD_STATE
        c_row = xc[:, c_off:c_off + SSD_STATE]
        hn = h0_ref[0, rows, :] * db[rows, :] + xb[rows, :] * b_row
        hn_ref[0, rows, :] = hn
        c8 = jnp.broadcast_to(c_row, (SUBLANES, SSD_STATE)).astype(BF16)
        y_parts.append(_dot_nt(c8, hn.astype(BF16))[0:1, :])
    xs = xc[:, 0:SSD_INNER]
    y = jnp.concatenate(y_parts, axis=1) + xs * dskip_ref[...]
    v = y * zs_ref[0]
    ms = jnp.mean(v * v, axis=-1, keepdims=True)
    o_ref[0] = (v * lax.rsqrt(ms + EPS) * gn_ref[...]).astype(o_ref.dtype)


def ssd_sample_step(cols, h0, xc, zs, dskip_row, gnorm):
    nb = h0.shape[0]
    rows = SSD_HEADS * SSD_HEAD_DIM
    const2 = lambda b: (0, 0)
    per_b = lambda b: (b, 0, 0)
    return pl.pallas_call(
        functools.partial(_ssd_s_step_kernel, nb=nb),
        grid=(nb,),
        in_specs=[
            pl.BlockSpec((SSD_INNER, LANES), const2),
            pl.BlockSpec((1, rows, SSD_STATE), per_b),
            pl.BlockSpec((1, 1, SSD_CONV_DIM), per_b),
            pl.BlockSpec((1, 1, SSD_INNER), per_b),
            pl.BlockSpec((1, SSD_INNER), const2),
            pl.BlockSpec((1, SSD_INNER), const2),
        ],
        out_specs=[pl.BlockSpec((1, rows, SSD_STATE), per_b), pl.BlockSpec((1, 1, SSD_INNER), per_b)],
        out_shape=[jax.ShapeDtypeStruct((nb, rows, SSD_STATE), F32), jax.ShapeDtypeStruct((nb, 1, SSD_INNER), BF16)],
        compiler_params=_cparams("parallel"),
        name="ssd_sample_step",
    )(cols, h0, xc.reshape(nb, 1, SSD_CONV_DIM), zs.reshape(nb, 1, SSD_INNER), dskip_row, gnorm)


def _block_diag_rows(row, n_heads, head_dim):
    rows = max(n_heads, SUBLANES)
    width = n_heads * head_dim
    r_i = lax.broadcasted_iota(jnp.int32, (rows, width), 0)
    c_i = lax.broadcasted_iota(jnp.int32, (rows, width), 1)
    return jnp.where(c_i // head_dim == r_i, jnp.broadcast_to(row, (rows, width)), 0.0)


def _fold_diag(full, n_heads, head_dim):
    rows, width = full.shape
    r_i = lax.broadcasted_iota(jnp.int32, (rows, width), 0)
    c_i = lax.broadcasted_iota(jnp.int32, (rows, width), 1)
    kept = jnp.where(c_i // head_dim == r_i, full, 0.0)
    out = kept[:, 0:head_dim]
    for h in range(1, n_heads):
        out = out + kept[:, h * head_dim:(h + 1) * head_dim]
    return out


def _fox_s_kernel(pt_ref, *refs, n_steps, scale):
    del pt_ref
    pp = PAGES_PER_STEP
    qrow_ref, qh_ref, kn_ref, vn_ref, lfn_ref, suf_ref = refs[0:6]
    k_refs = refs[6:6 + pp]
    v_refs = refs[6 + pp:6 + 2 * pp]
    lf_refs = refs[6 + 2 * pp:6 + 3 * pp]
    o_ref = refs[6 + 3 * pp]
    qbd, m_scr, l_scr, acc_scr, carry = refs[7 + 3 * pp:]
    j = pl.program_id(1)

    @pl.when(j == 0)
    def _():
        qbd[...] = _block_diag_rows(qrow_ref[0], FOX_HEADS, FOX_HEAD_DIM).astype(BF16)
        m_scr[...] = jnp.full_like(m_scr, -jnp.inf)
        l_scr[...] = jnp.zeros_like(l_scr)
        acc_scr[...] = jnp.zeros_like(acc_scr)
        carry[...] = lfn_ref[0]

    for i in range(pp):
        lf_t = lf_refs[i][0]
        r = _dot(jnp.concatenate(_split3(lf_t), axis=0), suf_ref[...])
        nh = FOX_HEADS
        page = lf_t.shape[1]
        suffix = r[0:nh, 0:page] + r[nh:2 * nh, 0:page] + r[2 * nh:3 * nh, 0:page]
        total = r[0:nh, page:page + 1] + r[nh:2 * nh, page:page + 1] + r[2 * nh:3 * nh, page:page + 1]
        bias = carry[...] + suffix
        carry[...] = carry[...] + total
        k = k_refs[i][0].astype(BF16)
        s = _dot_nt(qbd[...], k) * scale + bias
        m_old = m_scr[...]
        m_new = jnp.maximum(m_old, jnp.max(s, axis=-1, keepdims=True))
        p = jnp.exp(s - m_new)
        alpha = jnp.exp(m_old - m_new)
        l_scr[...] = alpha * l_scr[...] + jnp.sum(p, axis=-1, keepdims=True)
        acc_scr[...] = alpha * acc_scr[...] + _dot(p.astype(BF16), v_refs[i][0].astype(BF16))
        m_scr[...] = m_new

    @pl.when(j == n_steps - 1)
    def _():
        s_self = jnp.sum(qh_ref[0] * kn_ref[0], axis=-1, keepdims=True) * scale
        m_old = m_scr[...]
        m_new = jnp.maximum(m_old, s_self)
        alpha = jnp.exp(m_old - m_new)
        p_self = jnp.exp(s_self - m_new)
        l = alpha * l_scr[...] + p_self
        acc = alpha * _fold_diag(acc_scr[...], FOX_HEADS, FOX_HEAD_DIM) + p_self * vn_ref[0]
        o_ref[0] = (acc / l).astype(o_ref.dtype)


def fox_sample(page_table, q, k_new, v_new, lf_new, k_pool, v_pool, lf_pool_t):
    nb, n_pages = page_table.shape
    page = k_pool.shape[1]
    pp = PAGES_PER_STEP
    n_steps = n_pages // pp
    r_i = lax.broadcasted_iota(jnp.int32, (page, 2 * page), 0)
    c_i = lax.broadcasted_iota(jnp.int32, (page, 2 * page), 1)
    suf = jnp.where(c_i < page, r_i > c_i, True).astype(BF16)

    def paged(i, shape):
        return pl.BlockSpec((1,) + shape, lambda b, j, pt: (pt[b, n_pages - 1 - (j * pp + i)], 0, 0))

    per_b = lambda shape: pl.BlockSpec((1,) + shape, lambda b, j, pt: (b, 0, 0))
    in_specs = [per_b((1, FOX_WIDTH)), per_b((FOX_HEADS, FOX_HEAD_DIM)), per_b((FOX_HEADS, FOX_HEAD_DIM)),
                per_b((FOX_HEADS, FOX_HEAD_DIM)), per_b((FOX_HEADS, 1)),
                pl.BlockSpec((page, 2 * page), lambda b, j, pt: (0, 0))]
    in_specs += [paged(i, (page, FOX_WIDTH)) for i in range(pp)]
    in_specs += [paged(i, (page, FOX_WIDTH)) for i in range(pp)]
    in_specs += [paged(i, (FOX_HEADS, page)) for i in range(pp)]
    grid_spec = pltpu.PrefetchScalarGridSpec(
        num_scalar_prefetch=1,
        grid=(nb, n_steps),
        in_specs=in_specs,
        out_specs=per_b((FOX_HEADS, FOX_HEAD_DIM)),
        scratch_shapes=[
            pltpu.VMEM((FOX_HEADS, FOX_WIDTH), BF16),
            pltpu.VMEM((FOX_HEADS, 1), F32),
            pltpu.VMEM((FOX_HEADS, 1), F32),
            pltpu.VMEM((FOX_HEADS, FOX_WIDTH), F32),
            pltpu.VMEM((FOX_HEADS, 1), F32),
        ],
    )
    return pl.pallas_call(
        functools.partial(_fox_s_kernel, n_steps=n_steps, scale=FOX_HEAD_DIM ** -0.5),
        grid_spec=grid_spec,
        out_shape=jax.ShapeDtypeStruct((nb, FOX_HEADS, FOX_HEAD_DIM), BF16),
        compiler_params=_cparams("parallel", "arbitrary"),
        name="fox_sample",
    )(page_table, q.reshape(nb, 1, FOX_WIDTH), q.reshape(nb, FOX_HEADS, FOX_HEAD_DIM),
      k_new.reshape(nb, FOX_HEADS, FOX_HEAD_DIM), v_new.reshape(nb, FOX_HEADS, FOX_HEAD_DIM),
      lf_new.reshape(nb, FOX_HEADS, 1), suf,
      *([k_pool] * pp), *([v_pool] * pp), *([lf_pool_t] * pp))


def _mem_s_kernel(q_ref, k_ref, v_ref, o_ref, *, scale):
    qbd = _block_diag_rows(q_ref[0], MEM_HEADS, MEM_HEAD_DIM).astype(BF16)
    s = _dot_nt(qbd, k_ref[0].astype(BF16)) * scale
    e = jnp.exp(s - jnp.max(s, axis=-1, keepdims=True))
    p = e / jnp.sum(e, axis=-1, keepdims=True)
    full = _dot(p.astype(BF16), v_ref[0].astype(BF16))
    o_ref[0] = _fold_diag(full, MEM_HEADS, MEM_HEAD_DIM).astype(o_ref.dtype)


def mem_sample_attend(q, k, v):
    nb, n_mem, _ = k.shape
    per_b = lambda shape: pl.BlockSpec((1,) + shape, lambda b: (b, 0, 0))
    return pl.pallas_call(
        functools.partial(_mem_s_kernel, scale=MEM_HEAD_DIM ** -0.5),
        grid=(nb,),
        in_specs=[per_b((1, MEM_WIDTH)), per_b((n_mem, MEM_WIDTH)), per_b((n_mem, MEM_WIDTH))],
        out_specs=per_b((SUBLANES, MEM_HEAD_DIM)),
        out_shape=jax.ShapeDtypeStruct((nb, SUBLANES, MEM_HEAD_DIM), BF16),
        compiler_params=_cparams("parallel"),
        name="mem_sample_attend",
    )(q.reshape(nb, 1, MEM_WIDTH), k, v)


def _ffn_s_kernel(g_ref, u_ref, buf_ref, cw_ref, cb_ref, o_ref):
    cw = cw_ref[...]
    conv = buf_ref[0] * cw[0:1, :]
    conv = conv + buf_ref[1] * cw[1:2, :]
    conv = conv + g_ref[...] * cw[2:3, :]
    conv = conv + cb_ref[...]
    o_ref[...] = (_silu(conv) * u_ref[...]).astype(o_ref.dtype)


def ffn_sample_act(g, u, buf_t, conv_w, conv_b):
    full = lambda s: pl.BlockSpec(s, lambda i: (0,) * len(s))
    return pl.pallas_call(
        _ffn_s_kernel,
        grid=(1,),
        in_specs=[full(g.shape), full(u.shape), full(buf_t.shape), full(conv_w.shape), full(conv_b.shape)],
        out_specs=full(g.shape),
        out_shape=jax.ShapeDtypeStruct(g.shape, BF16),
        compiler_params=_cparams("arbitrary"),
        name="ffn_sample_act",
    )(g, u, buf_t, conv_w, conv_b)


def _project(xn, w, *, tm):
    mm = functools.partial(matmul, xn, tm=tm)
    (zs,) = mm(w["z"], tn=512, out_dtypes=(F32,), epilogue=lambda acc: (_silu(acc),), name="proj_z")
    (xbc,) = mm(w["xbc"], tn=512, out_dtypes=(F32,), name="proj_xbc")
    (dtf,) = mm(w["dtf"], tn=DTF_WIDTH, out_dtypes=(F32,), epilogue=_dtf_epilogue, rows=(w["dtf_bias"],),
                name="proj_dtf")
    (q,) = mm(w["q"], tn=512, out_dtypes=(BF16,), name="proj_q")
    k, k_bf = mm(w["k"], tn=512, out_dtypes=(F32, BF16), epilogue=lambda acc: (acc, acc), name="proj_k")
    v, v_bf = mm(w["v"], tn=512, out_dtypes=(F32, BF16), epilogue=lambda acc: (acc, acc), name="proj_v")
    (mq,) = mm(w["mq"], tn=512, out_dtypes=(BF16,), name="proj_mq")
    (gates,) = mm(w["gates"], tn=512, out_dtypes=(F32,), epilogue=lambda acc: (jax.nn.sigmoid(acc),),
                  name="proj_gates")
    return dict(zs=zs, xbc=xbc, dtf=dtf, q=q, k=k, k_bf=k_bf, v=v, v_bf=v_bf, mq=mq, gates=gates)


def kernel(x_prompt, x_sample, cache_fox_k, cache_fox_v, cache_fox_logf, cache_mem_k, cache_mem_v, state_ssd,
           state_ssd_conv, state_ffn_conv, page_table, mem_prompt, g_mix, w_in, ssd_conv_w, ssd_conv_b,
           ssd_dt_bias, ssd_a_log, ssd_d, ssd_g_norm, fox_b_forget, g_mem, w_mem_kv, w_branch, w_out, g_ffn,
           w_ffn_gate, w_ffn_up, ffn_conv_w, ffn_conv_b, w_ffn_down, g_final):
    depth = w_in.shape[0]
    assert depth == 1, "single-layer trunk"
    bp, seq, d = x_prompt.shape
    nb, dec_seq, _ = x_sample.shape
    assert d == D_MODEL and dec_seq == 1 and seq % SSD_CHUNK == 0
    n_mem = mem_prompt.shape[1]
    n_pool, page = cache_fox_k.shape[1], cache_fox_k.shape[2]
    assert page_table.shape[1] % PAGES_PER_STEP == 0
    mp = bp * seq

    wi = w_in[0]
    offs = [0]
    for width in (SSD_INNER, SSD_CONV_DIM, SSD_HEADS, FOX_WIDTH, FOX_WIDTH, FOX_WIDTH, FOX_HEADS, MEM_WIDTH,
                  N_BRANCH * D_MODEL):
        offs.append(offs[-1] + width)
    assert offs[-1] == wi.shape[1]
    seg = lambda i: wi[:, offs[i]:offs[i + 1]].astype(BF16)
    dtf_pad = DTF_WIDTH - SSD_HEADS - FOX_HEADS
    w = dict(
        z=seg(0), xbc=seg(1), q=seg(3), k=seg(4), v=seg(5), mq=seg(7), gates=seg(8),
        dtf=jnp.concatenate([seg(2), seg(6), jnp.zeros((D_MODEL, dtf_pad), BF16)], axis=1),
        dtf_bias=jnp.concatenate([ssd_dt_bias[0], fox_b_forget[0], jnp.zeros((dtf_pad,), F32)]).reshape(1, DTF_WIDTH),
    )
    wb = w_branch[0]
    w_s = wb[:SSD_INNER].astype(BF16)
    w_f = wb[SSD_INNER:SSD_INNER + FOX_WIDTH].astype(BF16)
    w_m = wb[SSD_INNER + FOX_WIDTH:].astype(BF16)
    w_o = w_out[0].astype(BF16)
    w_g = w_ffn_gate[0].astype(BF16)
    w_u = w_ffn_up[0].astype(BF16)
    w_d = w_ffn_down[0].astype(BF16)
    w_mk = w_mem_kv[0][:, :MEM_WIDTH].astype(BF16)
    w_mv = w_mem_kv[0][:, MEM_WIDTH:].astype(BF16)
    a_row = jnp.concatenate([-jnp.exp(ssd_a_log[0]), jnp.zeros((DTF_WIDTH - SSD_HEADS,), F32)]).reshape(1, DTF_WIDTH)
    dskip_row = jnp.repeat(ssd_d[0], SSD_HEAD_DIM).reshape(1, SSD_INNER)
    gnorm_row = ssd_g_norm[0].reshape(1, SSD_INNER)
    conv_w = ssd_conv_w[0]
    conv_b = ssd_conv_b[0].reshape(1, SSD_CONV_DIM)
    fconv_w = ffn_conv_w[0]
    fconv_b = ffn_conv_b[0].reshape(1, D_FF)

    xp = x_prompt.reshape(mp, D_MODEL)
    pr = _project(rms_norm(xp, g_mix[0], tm=512), w, tm=1024)

    mem_n = rms_norm(mem_prompt.reshape(bp * n_mem, D_MODEL), g_mem[0], tm=512)
    mk_p, mk_bf = matmul(mem_n, w_mk, tm=1024, tn=512, out_dtypes=(F32, BF16), epilogue=lambda acc: (acc, acc),
                         name="proj_mem_k")
    mv_p, mv_bf = matmul(mem_n, w_mv, tm=1024, tn=512, out_dtypes=(F32, BF16), epilogue=lambda acc: (acc, acc),
                         name="proj_mem_v")

    o_ssd_p, st_p = ssd_prompt(pr["xbc"], pr["dtf"], pr["zs"], conv_w, conv_b, a_row, dskip_row, gnorm_row,
                               batch=bp, seq=seq)

    blk = 256
    c_all = seq_cumsum(pr["dtf"], seq=seq)
    c_p = c_all[:, SSD_HEADS:SSD_HEADS + FOX_HEADS].reshape(bp, seq, FOX_HEADS).transpose(0, 2, 1)
    o_fox_p = fox_prompt(pr["q"], pr["k_bf"], pr["v_bf"], c_p.reshape(bp, FOX_HEADS, seq, 1),
                         c_p.reshape(bp, FOX_HEADS, seq // blk, 1, blk), batch=bp, seq=seq, blk=blk)
    o_mem_p = mem_prompt_attend(pr["mq"], mk_bf, mv_bf, batch=bp, seq=seq, n_mem=n_mem, tq=512)

    merged_p = branch_merge(o_ssd_p, o_fox_p, o_mem_p, w_s, w_f, w_m, pr["gates"], tm=512, tn=512)
    h_p, hn_p = out_proj(merged_p, w_o, xp, g_ffn[0], tm=256)
    act_p, fbuf_p = ffn_gate_up(hn_p, w_g, w_u, fconv_w, fconv_b, batch=bp, seq=seq, tm=1024, tn=512)
    y_p = ffn_down(act_p, w_d, h_p, g_final, tm=512, tk=512)

    xs_ = x_sample.reshape(nb, D_MODEL)
    sm = _project(rms_norm(xs_, g_mix[0], tm=nb), w, tm=nb)

    buf_t = state_ssd_conv[0].transpose(1, 0, 2)
    h_i = lax.broadcasted_iota(jnp.int32, (DTF_WIDTH, SSD_INNER), 0)
    col_i = lax.broadcasted_iota(jnp.int32, (DTF_WIDTH, SSD_INNER), 1)
    expand = (col_i // SSD_HEAD_DIM == h_i).astype(BF16)
    xc_s, cols_s = ssd_sample_prep(sm["xbc"], buf_t, conv_w, conv_b, sm["dtf"], a_row, expand)
    st_s, o_ssd_s = ssd_sample_step(cols_s, state_ssd[0].reshape(nb, SSD_HEADS * SSD_HEAD_DIM, SSD_STATE), xc_s,
                                    sm["zs"], dskip_row, gnorm_row)

    lf_s = sm["dtf"][:, SSD_HEADS:SSD_HEADS + FOX_HEADS]
    o_fox_s = fox_sample(page_table, sm["q"].astype(F32), sm["k"], sm["v"], lf_s,
                         cache_fox_k[0].reshape(n_pool, page, FOX_WIDTH),
                         cache_fox_v[0].reshape(n_pool, page, FOX_WIDTH),
                         cache_fox_logf[0].transpose(0, 2, 1))
    o_mem_s = mem_sample_attend(sm["mq"].astype(F32), cache_mem_k[0].reshape(nb, n_mem, MEM_WIDTH),
                                cache_mem_v[0].reshape(nb, n_mem, MEM_WIDTH))
    o_mem_s = o_mem_s[:, :MEM_HEADS].reshape(nb, MEM_WIDTH)

    merged_s = branch_merge(o_ssd_s.reshape(nb, SSD_INNER), o_fox_s.reshape(nb, FOX_WIDTH), o_mem_s, w_s, w_f, w_m,
                            sm["gates"], tm=nb, tn=512)
    h_s, hn_s = out_proj(merged_s, w_o, xs_, g_ffn[0], tm=nb)
    (g_s,) = matmul(hn_s, w_g, tm=nb, tn=512, out_dtypes=(F32,), name="ffn_gate_s")
    (u_s,) = matmul(hn_s, w_u, tm=nb, tn=512, out_dtypes=(F32,), name="ffn_up_s")
    act_s = ffn_sample_act(g_s, u_s, state_ffn_conv[0].transpose(1, 0, 2), fconv_w, fconv_b)
    y_s = ffn_down(act_s, w_d, h_s, g_final, tm=nb, tk=512)

    lf_p = pr["dtf"][:, SSD_HEADS:SSD_HEADS + FOX_HEADS]
    heads5 = lambda t, b, l: t.reshape(1, b, l, FOX_HEADS, FOX_HEAD_DIM)
    return (
        y_p.reshape(bp, seq, D_MODEL),
        y_s.reshape(nb, 1, D_MODEL),
        heads5(pr["k"], bp, seq),
        heads5(pr["v"], bp, seq),
        lf_p.reshape(1, bp, seq, FOX_HEADS),
        mk_p.reshape(1, bp, n_mem, MEM_HEADS, MEM_HEAD_DIM),
        mv_p.reshape(1, bp, n_mem, MEM_HEADS, MEM_HEAD_DIM),
        st_p.reshape(1, bp, SSD_HEADS, SSD_HEAD_DIM, SSD_STATE),
        pr["xbc"].reshape(bp, seq, SSD_CONV_DIM)[:, seq - (SSD_CONV - 1):][None],
        fbuf_p[None],
        heads5(sm["k"], nb, 1),
        heads5(sm["v"], nb, 1),
        lf_s.reshape(1, nb, 1, FOX_HEADS),
        st_s.reshape(1, nb, SSD_HEADS, SSD_HEAD_DIM, SSD_STATE),
        jnp.concatenate([state_ssd_conv[0][:, 1:], sm["xbc"][:, None, :]], axis=1)[None],
        jnp.concatenate([state_ffn_conv[0][:, 1:], g_s[:, None, :]], axis=1)[None],
    )
```

```python
import functools
import math

import jax
import jax.numpy as jnp
from jax import lax
from jax.experimental import pallas as pl
from jax.experimental.pallas import tpu as pltpu

F32 = jnp.float32
BF16 = jnp.bfloat16

D_MODEL = 2048
SSD_INNER = 4096
SSD_HEAD_DIM = 64
SSD_HEADS = 64
SSD_GROUPS = 8
SSD_STATE = 128
SSD_CONV = 4
SSD_CHUNK = 128
SSD_CONV_DIM = SSD_INNER + 2 * SSD_GROUPS * SSD_STATE
HEADS_PER_GROUP = SSD_HEADS // SSD_GROUPS
GROUP_WIDTH = HEADS_PER_GROUP * SSD_HEAD_DIM
FOX_HEAD_DIM = 128
FOX_HEADS = 16
FOX_WIDTH = FOX_HEADS * FOX_HEAD_DIM
MEM_HEADS = 4
MEM_HEAD_DIM = 128
MEM_WIDTH = MEM_HEADS * MEM_HEAD_DIM
D_FF = 5632
FFN_CONV = 3
N_BRANCH = 3
EPS = 1e-6

LANES = 128
SUBLANES = 8
VMEM_LIMIT_BYTES = 56 * 1024 * 1024

DTF_WIDTH = LANES
PAGES_PER_STEP = 8


def _cparams(*semantics):
    return pltpu.CompilerParams(dimension_semantics=semantics, vmem_limit_bytes=VMEM_LIMIT_BYTES)


def _dot(a, b):
    return jnp.dot(a, b, preferred_element_type=F32)


def _dot_nt(a, b):
    return lax.dot_general(a, b, (((1,), (1,)), ((), ())), preferred_element_type=F32)


def _silu(x):
    return x * jax.nn.sigmoid(x)


def _softplus(x):
    return jnp.maximum(x, 0.0) + jnp.log1p(jnp.exp(-jnp.abs(x)))


def _split3(x):
    hi = x.astype(BF16)
    r1 = x - hi.astype(F32)
    mid = r1.astype(BF16)
    lo = (r1 - mid.astype(F32)).astype(BF16)
    return hi, mid, lo


def _rms_kernel(x_ref, g_ref, o_ref):
    x = x_ref[...]
    ms = jnp.mean(x * x, axis=-1, keepdims=True)
    o_ref[...] = (x * lax.rsqrt(ms + EPS) * g_ref[...]).astype(o_ref.dtype)


def rms_norm(x, g, *, tm, out_dtype=BF16):
    m, d = x.shape
    return pl.pallas_call(
        _rms_kernel,
        grid=(m // tm,),
        in_specs=[pl.BlockSpec((tm, d), lambda i: (i, 0)), pl.BlockSpec((1, d), lambda i: (0, 0))],
        out_specs=pl.BlockSpec((tm, d), lambda i: (i, 0)),
        out_shape=jax.ShapeDtypeStruct((m, d), out_dtype),
        compiler_params=_cparams("parallel"),
        name="rms_norm",
    )(x, g.reshape(1, d))


def _mm_kernel(*refs, epilogue, n_rows):
    a_ref, w_ref = refs[0], refs[1]
    rows = [r[...] for r in refs[2:2 + n_rows]]
    outs = refs[2 + n_rows:]
    acc = _dot(a_ref[...], w_ref[...])
    vals = epilogue(acc, *rows)
    for o, v in zip(outs, vals):
        o[...] = v.astype(o.dtype)


def matmul(a, w, *, tm, tn, out_dtypes, epilogue=lambda acc: (acc,), rows=(), name="matmul"):
    m, k = a.shape
    n = w.shape[1]
    tm = min(tm, m)
    tn = min(tn, n)
    in_specs = [pl.BlockSpec((tm, k), lambda i, j: (i, 0)), pl.BlockSpec((k, tn), lambda i, j: (0, j))]
    in_specs += [pl.BlockSpec((1, tn), lambda i, j: (0, j)) for _ in rows]
    outs = pl.pallas_call(
        functools.partial(_mm_kernel, epilogue=epilogue, n_rows=len(rows)),
        grid=(m // tm, n // tn),
        in_specs=in_specs,
        out_specs=[pl.BlockSpec((tm, tn), lambda i, j: (i, j)) for _ in out_dtypes],
        out_shape=[jax.ShapeDtypeStruct((m, n), dt) for dt in out_dtypes],
        compiler_params=_cparams("parallel", "arbitrary"),
        name=name,
    )(a, w, *rows)
    return outs


def _dtf_epilogue(acc, bias):
    x = acc + bias
    lane = lax.broadcasted_iota(jnp.int32, x.shape, 1)
    return (jnp.where(lane < SSD_HEADS, _softplus(x), -_softplus(-x)),)


def _cumsum_kernel(x_ref, o_ref):
    x = x_ref[...]
    n = x.shape[0]
    row = lax.broadcasted_iota(jnp.int32, x.shape, 0)
    s = 1
    while s < n:
        x = x + jnp.where(row >= s, pltpu.roll(x, s, 0), 0.0)
        s *= 2
    o_ref[...] = x


def seq_cumsum(x, *, seq):
    m, d = x.shape
    return pl.pallas_call(
        _cumsum_kernel,
        grid=(m // seq,),
        in_specs=[pl.BlockSpec((seq, d), lambda b: (b, 0))],
        out_specs=pl.BlockSpec((seq, d), lambda b: (b, 0)),
        out_shape=jax.ShapeDtypeStruct((m, d), F32),
        compiler_params=_cparams("parallel"),
        name="seq_cumsum",
    )(x)


def _pair_cols(mat, pair, lane_lo):
    a = mat[:, 2 * pair:2 * pair + 1]
    b = mat[:, 2 * pair + 1:2 * pair + 2]
    return jnp.where(lane_lo, a, b)


def _ssd_kernel(xbc_ref, dtf_ref, zs_ref, cw_ref, cb_ref, a_ref, dskip_ref, gn_ref,
                o_ref, st_out_ref, xbuf, xc, st, y_scr, xw_scr):
    c = pl.program_id(1)
    nc = pl.num_programs(1)
    cs = SSD_CHUNK
    halo = SUBLANES

    @pl.when(c == 0)
    def _():
        xbuf[0:halo, :] = jnp.zeros((halo, SSD_CONV_DIM), F32)
        st[...] = jnp.zeros_like(st)

    @pl.when(c > 0)
    def _():
        xbuf[0:halo, :] = xbuf[cs:cs + halo, :]

    xbuf[halo:halo + cs, :] = xbc_ref[...]

    cw = cw_ref[...]
    cb = cb_ref[...]
    col_chunk = 512
    for j in range(SSD_CONV_DIM // col_chunk):
        sl = slice(j * col_chunk, (j + 1) * col_chunk)
        acc = xbuf[halo - 3:halo - 3 + cs, sl] * cw[0:1, sl]
        for i in range(1, SSD_CONV):
            acc = acc + xbuf[halo - 3 + i:halo - 3 + i + cs, sl] * cw[i:i + 1, sl]
        xc[:, sl] = _silu(acc + cb[:, sl])

    dt = dtf_ref[...]
    da = dt * a_ref[...]
    r_i = lax.broadcasted_iota(jnp.int32, (cs, cs), 0)
    c_i = lax.broadcasted_iota(jnp.int32, (cs, cs), 1)
    causal = c_i <= r_i
    tril = jnp.where(causal, 1.0, 0.0)
    a_cs = jnp.dot(tril, da, preferred_element_type=F32, precision=lax.Precision.HIGHEST)
    a_cs_t = a_cs.T
    dt_t = dt.T
    last = a_cs[cs - 1:cs, :]
    ecs = jnp.exp(a_cs)
    wfac = dt * jnp.exp(last - a_cs)
    cdec = jnp.exp(last)

    lane_lo = lax.broadcasted_iota(jnp.int32, (cs, LANES), 1) < SSD_HEAD_DIM
    lane_lo_row = lane_lo[0:1, :]
    pairs_per_group = HEADS_PER_GROUP // 2
    for g in range(SSD_GROUPS):
        b_off = SSD_INNER + g * SSD_STATE
        c_off = SSD_INNER + SSD_GROUPS * SSD_STATE + g * SSD_STATE
        bg = xc[:, b_off:b_off + SSD_STATE]
        cg = xc[:, c_off:c_off + SSD_STATE].astype(BF16)
        cbm = _dot_nt(cg, bg.astype(BF16))
        yoff = _dot(cg, st[g].astype(BF16))
        cd_tiles = []
        for pr in range(pairs_per_group):
            pair = g * pairs_per_group + pr
            col0 = pair * LANES
            xs_pair = xc[:, col0:col0 + LANES]
            xs_bf = xs_pair.astype(BF16)
            yd = []
            for h in (2 * pair, 2 * pair + 1):
                seg = a_cs[:, h:h + 1] - a_cs_t[h:h + 1, :]
                decay = jnp.exp(jnp.where(causal, seg, -jnp.inf))
                mh = (cbm * decay * dt_t[h:h + 1, :]).astype(BF16)
                yd.append(_dot(mh, xs_bf))
            y_pair = jnp.where(lane_lo, yd[0], yd[1])
            y_pair = y_pair + _pair_cols(ecs, pair, lane_lo) * yoff[:, pr * LANES:(pr + 1) * LANES]
            y_pair = y_pair + xs_pair * dskip_ref[:, col0:col0 + LANES]
            y_scr[:, col0:col0 + LANES] = y_pair
            xw_scr[:, pr * LANES:(pr + 1) * LANES] = (xs_pair * _pair_cols(wfac, pair, lane_lo)).astype(BF16)
            cd_tiles.append(_pair_cols(cdec, pair, lane_lo_row))
        cd_row = jnp.concatenate(cd_tiles, axis=1)
        st[g] = st[g] * cd_row + _dot(bg.T.astype(BF16), xw_scr[...])

    v = y_scr[...] * zs_ref[...]
    ms = jnp.mean(v * v, axis=-1, keepdims=True)
    o_ref[...] = (v * lax.rsqrt(ms + EPS) * gn_ref[...]).astype(o_ref.dtype)

    @pl.when(c == nc - 1)
    def _():
        for g in range(SSD_GROUPS):
            st_out_ref[0, g * GROUP_WIDTH:(g + 1) * GROUP_WIDTH, :] = st[g].T


def ssd_prompt(xbc, dtf, zs, conv_w, conv_b, a_row, dskip_row, gnorm, *, batch, seq):
    nc = seq // SSD_CHUNK
    cs = SSD_CHUNK
    row = lambda b, c: (b * nc + c, 0)
    const = lambda b, c: (0, 0)
    return pl.pallas_call(
        _ssd_kernel,
        grid=(batch, nc),
        in_specs=[
            pl.BlockSpec((cs, SSD_CONV_DIM), row),
            pl.BlockSpec((cs, DTF_WIDTH), row),
            pl.BlockSpec((cs, SSD_INNER), row),
            pl.BlockSpec((SSD_CONV, SSD_CONV_DIM), const),
            pl.BlockSpec((1, SSD_CONV_DIM), const),
            pl.BlockSpec((1, DTF_WIDTH), const),
            pl.BlockSpec((1, SSD_INNER), const),
            pl.BlockSpec((1, SSD_INNER), const),
        ],
        out_specs=[
            pl.BlockSpec((cs, SSD_INNER), row),
            pl.BlockSpec((1, SSD_HEADS * SSD_HEAD_DIM, SSD_STATE), lambda b, c: (b, 0, 0)),
        ],
        out_shape=[
            jax.ShapeDtypeStruct((batch * seq, SSD_INNER), BF16),
            jax.ShapeDtypeStruct((batch, SSD_HEADS * SSD_HEAD_DIM, SSD_STATE), F32),
        ],
        scratch_shapes=[
            pltpu.VMEM((cs + SUBLANES, SSD_CONV_DIM), F32),
            pltpu.VMEM((cs, SSD_CONV_DIM), F32),
            pltpu.VMEM((SSD_GROUPS, SSD_STATE, GROUP_WIDTH), F32),
            pltpu.VMEM((cs, SSD_INNER), F32),
            pltpu.VMEM((cs, GROUP_WIDTH), BF16),
        ],
        compiler_params=_cparams("parallel", "arbitrary"),
        name="ssd_prompt",
    )(xbc, dtf, zs, conv_w, conv_b, a_row, dskip_row, gnorm)


def _fox_kernel(q_ref, k_ref, v_ref, cq_ref, ck_ref, o_ref, *, tq, tk, heads, scale):
    qi = pl.program_id(2)
    dh = FOX_HEAD_DIM
    ratio = tq // tk
    qs = [q_ref[:, h * dh:(h + 1) * dh] for h in range(heads)]
    cqs = [cq_ref[0, h] for h in range(heads)]

    def step(j, carry, masked):
        off = pl.multiple_of(j * tk, tk)
        kk = k_ref[pl.ds(off, tk), :]
        vv = v_ref[pl.ds(off, tk), :]
        if masked:
            r_i = lax.broadcasted_iota(jnp.int32, (tq, tk), 0) + qi * tq
            c_i = lax.broadcasted_iota(jnp.int32, (tq, tk), 1) + j * tk
            visible = c_i <= r_i
        out = []
        for h in range(heads):
            m, l, acc = carry[3 * h:3 * h + 3]
            s = _dot_nt(qs[h], kk[:, h * dh:(h + 1) * dh]) * scale
            s = s + cqs[h] - ck_ref[0, h, j]
            if masked:
                s = jnp.where(visible, s, -jnp.inf)
            m_new = jnp.maximum(m, jnp.max(s, axis=-1, keepdims=True))
            p = jnp.exp(s - m_new)
            alpha = jnp.exp(m - m_new)
            l = alpha * l + jnp.sum(p, axis=-1, keepdims=True)
            acc = alpha * acc + _dot(p.astype(BF16), vv[:, h * dh:(h + 1) * dh])
            out += [m_new, l, acc]
        return tuple(out)

    init = (jnp.full((tq, 1), -jnp.inf, F32), jnp.zeros((tq, 1), F32), jnp.zeros((tq, dh), F32)) * heads
    carry = lax.fori_loop(0, qi * ratio, lambda j, cr: step(j, cr, False), init)
    for jj in range(ratio):
        carry = step(qi * ratio + jj, carry, True)
    for h in range(heads):
        _, l, acc = carry[3 * h:3 * h + 3]
        o_ref[:, h * dh:(h + 1) * dh] = (acc / l).astype(o_ref.dtype)


def fox_prompt(q, k, v, cq, ck, *, batch, seq, tq, tk, heads):
    nq = seq // tq
    width = heads * FOX_HEAD_DIM
    return pl.pallas_call(
        functools.partial(_fox_kernel, tq=tq, tk=tk, heads=heads, scale=FOX_HEAD_DIM ** -0.5),
        grid=(batch, FOX_HEADS // heads, nq),
        in_specs=[
            pl.BlockSpec((tq, width), lambda b, h, i: (b * nq + i, h)),
            pl.BlockSpec((seq, width), lambda b, h, i: (b, h)),
            pl.BlockSpec((seq, width), lambda b, h, i: (b, h)),
            pl.BlockSpec((1, heads, tq, 1), lambda b, h, i: (b, h, i, 0)),
            pl.BlockSpec((1, heads, seq // tk, 1, tk), lambda b, h, i: (b, h, 0, 0, 0)),
        ],
        out_specs=pl.BlockSpec((tq, width), lambda b, h, i: (b * nq + i, h)),
        out_shape=jax.ShapeDtypeStruct((batch * seq, FOX_WIDTH), BF16),
        compiler_params=_cparams("parallel", "parallel", "arbitrary"),
        name="fox_prompt",
    )(q, k, v, cq, ck)


def _mem_kernel(q_ref, k_ref, v_ref, o_ref, *, scale):
    s = _dot_nt(q_ref[...], k_ref[...]) * scale
    e = jnp.exp(s - jnp.max(s, axis=-1, keepdims=True))
    p = e / jnp.sum(e, axis=-1, keepdims=True)
    o_ref[...] = _dot(p.astype(BF16), v_ref[...]).astype(o_ref.dtype)


def mem_prompt_attend(q, k, v, *, batch, seq, n_mem, tq):
    nq = seq // tq
    return pl.pallas_call(
        functools.partial(_mem_kernel, scale=MEM_HEAD_DIM ** -0.5),
        grid=(batch, MEM_HEADS, nq),
        in_specs=[
            pl.BlockSpec((tq, MEM_HEAD_DIM), lambda b, h, i: (b * nq + i, h)),
            pl.BlockSpec((n_mem, MEM_HEAD_DIM), lambda b, h, i: (b, h)),
            pl.BlockSpec((n_mem, MEM_HEAD_DIM), lambda b, h, i: (b, h)),
        ],
        out_specs=pl.BlockSpec((tq, MEM_HEAD_DIM), lambda b, h, i: (b * nq + i, h)),
        out_shape=jax.ShapeDtypeStruct((batch * seq, MEM_WIDTH), BF16),
        compiler_params=_cparams("parallel", "parallel", "arbitrary"),
        name="mem_prompt_attend",
    )(q, k, v)


def _merge_kernel(os_ref, of_ref, om_ref, ws_ref, wf_ref, wm_ref, g0_ref, g1_ref, g2_ref, o_ref):
    u = g0_ref[...] * _dot(os_ref[...], ws_ref[...])
    u = u + g1_ref[...] * _dot(of_ref[...], wf_ref[...])
    u = u + g2_ref[...] * _dot(om_ref[...], wm_ref[...])
    o_ref[...] = u.astype(o_ref.dtype)


def branch_merge(o_ssd, o_fox, o_mem, w_s, w_f, w_m, gates, *, tm, tn):
    m = o_ssd.shape[0]
    tm = min(tm, m)
    nj = D_MODEL // tn
    act = lambda k: pl.BlockSpec((tm, k), lambda i, j: (i, 0))
    wgt = lambda k: pl.BlockSpec((k, tn), lambda i, j: (0, j))
    gate = lambda br: pl.BlockSpec((tm, tn), lambda i, j: (i, br * nj + j))
    return pl.pallas_call(
        _merge_kernel,
        grid=(m // tm, nj),
        in_specs=[act(SSD_INNER), act(FOX_WIDTH), act(MEM_WIDTH), wgt(SSD_INNER), wgt(FOX_WIDTH), wgt(MEM_WIDTH),
                  gate(0), gate(1), gate(2)],
        out_specs=pl.BlockSpec((tm, tn), lambda i, j: (i, j)),
        out_shape=jax.ShapeDtypeStruct((m, D_MODEL), BF16),
        compiler_params=_cparams("parallel", "arbitrary"),
        name="branch_merge",
    )(o_ssd, o_fox, o_mem, w_s, w_f, w_m, gates, gates, gates)


def _outproj_kernel(a_ref, w_ref, x_ref, g_ref, h_ref, hn_ref):
    h = x_ref[...] + _dot(a_ref[...], w_ref[...])
    h_ref[...] = h
    ms = jnp.mean(h * h, axis=-1, keepdims=True)
    hn_ref[...] = (h * lax.rsqrt(ms + EPS) * g_ref[...]).astype(hn_ref.dtype)


def out_proj(merged, w_out, x, g_ffn, *, tm):
    m = x.shape[0]
    tm = min(tm, m)
    row = lambda i: (i, 0)
    const = lambda i: (0, 0)
    return pl.pallas_call(
        _outproj_kernel,
        grid=(m // tm,),
        in_specs=[pl.BlockSpec((tm, D_MODEL), row), pl.BlockSpec((D_MODEL, D_MODEL), const),
                  pl.BlockSpec((tm, D_MODEL), row), pl.BlockSpec((1, D_MODEL), const)],
        out_specs=[pl.BlockSpec((tm, D_MODEL), row), pl.BlockSpec((tm, D_MODEL), row)],
        out_shape=[jax.ShapeDtypeStruct((m, D_MODEL), F32), jax.ShapeDtypeStruct((m, D_MODEL), BF16)],
        compiler_params=_cparams("parallel"),
        name="out_proj",
    )(merged, w_out, x, g_ffn.reshape(1, D_MODEL))


def _ffn_kernel(hn_ref, wg_ref, wu_ref, cw_ref, cb_ref, act_ref, buf_ref, gbuf, *, tiles_per_seq):
    i = pl.program_id(1)
    tm = hn_ref.shape[0]
    halo = SUBLANES

    @pl.when(i % tiles_per_seq == 0)
    def _():
        gbuf[0:halo, :] = jnp.zeros((halo, gbuf.shape[1]), F32)

    @pl.when(i % tiles_per_seq != 0)
    def _():
        gbuf[0:halo, :] = gbuf[tm:tm + halo, :]

    hn = hn_ref[...]
    gbuf[halo:halo + tm, :] = _dot(hn, wg_ref[...])
    cw = cw_ref[...]
    conv = gbuf[halo - 2:halo - 2 + tm, :] * cw[0:1, :]
    conv = conv + gbuf[halo - 1:halo - 1 + tm, :] * cw[1:2, :]
    conv = conv + gbuf[halo:halo + tm, :] * cw[2:3, :]
    conv = conv + cb_ref[...]
    act_ref[...] = (_silu(conv) * _dot(hn, wu_ref[...])).astype(act_ref.dtype)
    buf_ref[0] = gbuf[halo + tm - 2:halo + tm, :]


def ffn_gate_up(hn, w_gate, w_up, conv_w, conv_b, *, batch, seq, tm, tn):
    m = hn.shape[0]
    tiles_per_seq = seq // tm
    return pl.pallas_call(
        functools.partial(_ffn_kernel, tiles_per_seq=tiles_per_seq),
        grid=(D_FF // tn, m // tm),
        in_specs=[
            pl.BlockSpec((tm, D_MODEL), lambda j, i: (i, 0)),
            pl.BlockSpec((D_MODEL, tn), lambda j, i: (0, j)),
            pl.BlockSpec((D_MODEL, tn), lambda j, i: (0, j)),
            pl.BlockSpec((FFN_CONV, tn), lambda j, i: (0, j)),
            pl.BlockSpec((1, tn), lambda j, i: (0, j)),
        ],
        out_specs=[
            pl.BlockSpec((tm, tn), lambda j, i: (i, j)),
            pl.BlockSpec((1, FFN_CONV - 1, tn), lambda j, i: (i // tiles_per_seq, 0, j)),
        ],
        out_shape=[
            jax.ShapeDtypeStruct((m, D_FF), BF16),
            jax.ShapeDtypeStruct((batch, FFN_CONV - 1, D_FF), F32),
        ],
        scratch_shapes=[pltpu.VMEM((tm + SUBLANES, tn), F32)],
        compiler_params=_cparams("parallel", "arbitrary"),
        name="ffn_gate_up",
    )(hn, w_gate, w_up, conv_w, conv_b)


def _down_kernel(a_ref, w_ref, h_ref, g_ref, y_ref, acc):
    k = pl.program_id(1)

    @pl.when(k == 0)
    def _():
        acc[...] = h_ref[...]

    acc[...] += _dot(a_ref[...], w_ref[...])

    @pl.when(k == pl.num_programs(1) - 1)
    def _():
        h = acc[...]
        ms = jnp.mean(h * h, axis=-1, keepdims=True)
        y_ref[...] = h * lax.rsqrt(ms + EPS) * g_ref[...]


def ffn_down(act, w_down, h, g_final, *, tm, tk):
    m = h.shape[0]
    tm = min(tm, m)
    return pl.pallas_call(
        _down_kernel,
        grid=(m // tm, D_FF // tk),
        in_specs=[pl.BlockSpec((tm, tk), lambda i, k: (i, k)), pl.BlockSpec((tk, D_MODEL), lambda i, k: (k, 0)),
                  pl.BlockSpec((tm, D_MODEL), lambda i, k: (i, 0)), pl.BlockSpec((1, D_MODEL), lambda i, k: (0, 0))],
        out_specs=pl.BlockSpec((tm, D_MODEL), lambda i, k: (i, 0)),
        out_shape=jax.ShapeDtypeStruct((m, D_MODEL), F32),
        scratch_shapes=[pltpu.VMEM((tm, D_MODEL), F32)],
        compiler_params=_cparams("parallel", "arbitrary"),
        name="ffn_down",
    )(act, w_down, h, g_final.reshape(1, D_MODEL))


def _ssd_s_prep_kernel(xbc_ref, buf_ref, cw_ref, cb_ref, dtf_ref, a_ref, expand_ref, xc_ref, cols_ref):
    cw = cw_ref[...]
    acc = buf_ref[0] * cw[0:1, :]
    acc = acc + buf_ref[1] * cw[1:2, :]
    acc = acc + buf_ref[2] * cw[2:3, :]
    acc = acc + xbc_ref[...] * cw[3:4, :]
    xcv = _silu(acc + cb_ref[...])
    xc_ref[...] = xcv
    dt = dtf_ref[...]
    decay = jnp.exp(dt * a_ref[...])
    nb = dt.shape[0]
    parts = _split3(dt) + _split3(decay)
    stacked = jnp.concatenate(parts, axis=0)
    ex = _dot(stacked, expand_ref[...])
    dt_exp = ex[0:nb] + ex[nb:2 * nb] + ex[2 * nb:3 * nb]
    dec_exp = ex[3 * nb:4 * nb] + ex[4 * nb:5 * nb] + ex[5 * nb:6 * nb]
    xdt = xcv[:, 0:SSD_INNER] * dt_exp
    pad = jnp.zeros((LANES - 2 * nb, SSD_INNER), F32)
    cols_ref[...] = jnp.concatenate([xdt, dec_exp, pad], axis=0).T


def ssd_sample_prep(xbc, buf_t, conv_w, conv_b, dtf, a_row, expand):
    nb = xbc.shape[0]
    full = lambda s: pl.BlockSpec(s, lambda i: (0,) * len(s))
    return pl.pallas_call(
        _ssd_s_prep_kernel,
        grid=(1,),
        in_specs=[full(xbc.shape), full(buf_t.shape), full(conv_w.shape), full(conv_b.shape), full(dtf.shape),
                  full(a_row.shape), full(expand.shape)],
        out_specs=[full((nb, SSD_CONV_DIM)), full((SSD_INNER, LANES))],
        out_shape=[jax.ShapeDtypeStruct((nb, SSD_CONV_DIM), F32), jax.ShapeDtypeStruct((SSD_INNER, LANES), F32)],
        compiler_params=_cparams("arbitrary"),
        name="ssd_sample_prep",
    )(xbc, buf_t, conv_w, conv_b, dtf, a_row, expand)


def _ssd_s_step_kernel(cols_ref, h0_ref, xc_ref, zs_ref, dskip_ref, gn_ref, hn_ref, o_ref, *, nb):
    b = pl.program_id(0)
    cols = cols_ref[...]
    hi, mid, lo = _split3(cols)
    lhs = jnp.concatenate([hi, mid, lo], axis=1)
    r_i = lax.broadcasted_iota(jnp.int32, (3 * LANES, 2 * LANES), 0) % LANES
    c_i = lax.broadcasted_iota(jnp.int32, (3 * LANES, 2 * LANES), 1)
    sel = jnp.where(r_i == jnp.where(c_i < LANES, b, nb + b), 1.0, 0.0).astype(BF16)
    picked = _dot(lhs, sel)
    xb = picked[:, 0:LANES]
    db = picked[:, LANES:2 * LANES]
    xc = xc_ref[0]
    y_parts = []
    for g in range(SSD_GROUPS):
        rows = slice(g * GROUP_WIDTH, (g + 1) * GROUP_WIDTH)
        b_row = xc[:, SSD_INNER + g * SSD_STATE:SSD_INNER + (g + 1) * SSD_STATE]
        c_off = SSD_INNER + SSD_GROUPS * SSD_STATE + g * SSD_STATE
        c_row = xc[:, c_off:c_off + SSD_STATE]
        hn = h0_ref[0, rows, :] * db[rows, :] + xb[rows, :] * b_row
        hn_ref[0, rows, :] = hn
        c8 = jnp.broadcast_to(c_row, (SUBLANES, SSD_STATE)).astype(BF16)
        y_parts.append(_dot_nt(c8, hn.astype(BF16))[0:1, :])
    xs = xc[:, 0:SSD_INNER]
    y = jnp.concatenate(y_parts, axis=1) + xs * dskip_ref[...]
    v = y * zs_ref[0]
    ms = jnp.mean(v * v, axis=-1, keepdims=True)
    o_ref[0] = (v * lax.rsqrt(ms + EPS) * gn_ref[...]).astype(o_ref.dtype)


def ssd_sample_step(cols, h0, xc, zs, dskip_row, gnorm):
    nb = h0.shape[0]
    rows = SSD_HEADS * SSD_HEAD_DIM
    const2 = lambda b: (0, 0)
    per_b = lambda b: (b, 0, 0)
    return pl.pallas_call(
        functools.partial(_ssd_s_step_kernel, nb=nb),
        grid=(nb,),
        in_specs=[
            pl.BlockSpec((SSD_INNER, LANES), const2),
            pl.BlockSpec((1, rows, SSD_STATE), per_b),
            pl.BlockSpec((1, 1, SSD_CONV_DIM), per_b),
            pl.BlockSpec((1, 1, SSD_INNER), per_b),
            pl.BlockSpec((1, SSD_INNER), const2),
            pl.BlockSpec((1, SSD_INNER), const2),
        ],
        out_specs=[pl.BlockSpec((1, rows, SSD_STATE), per_b), pl.BlockSpec((1, 1, SSD_INNER), per_b)],
        out_shape=[jax.ShapeDtypeStruct((nb, rows, SSD_STATE), F32), jax.ShapeDtypeStruct((nb, 1, SSD_INNER), BF16)],
        compiler_params=_cparams("parallel"),
        name="ssd_sample_step",
    )(cols, h0, xc.reshape(nb, 1, SSD_CONV_DIM), zs.reshape(nb, 1, SSD_INNER), dskip_row, gnorm)


def _head_matched(shape, n_heads):
    r_i = lax.broadcasted_iota(jnp.int32, shape, 0)
    c_i = lax.broadcasted_iota(jnp.int32, shape, 1)
    return c_i % n_heads == r_i % n_heads


def _fox_s_kernel(pt_ref, *refs, n_steps, scale):
    del pt_ref
    pp = PAGES_PER_STEP
    nh = FOX_HEADS
    q_ref, kn_ref, vn_ref, lfn_ref, suf_ref = refs[0:5]
    k_refs = refs[5:5 + pp]
    v_refs = refs[5 + pp:5 + 2 * pp]
    lf_refs = refs[5 + 2 * pp:5 + 3 * pp]
    o_ref = refs[5 + 3 * pp]
    m_scr, l_scr, acc_scr, carry = refs[6 + 3 * pp:]
    j = pl.program_id(1)

    @pl.when(j == 0)
    def _():
        m_scr[...] = jnp.full_like(m_scr, -jnp.inf)
        l_scr[...] = jnp.zeros_like(l_scr)
        acc_scr[...] = jnp.zeros_like(acc_scr)
        carry[...] = lfn_ref[0]

    q = q_ref[0].astype(BF16)
    page = k_refs[0].shape[2]
    rows = lf_refs[0].shape[1]
    keep = _head_matched((nh, page * nh), nh)
    row_i = lax.broadcasted_iota(jnp.int32, (rows, LANES), 0)

    run = carry[...]
    scores = []
    for i in range(pp):
        lfd = lf_refs[i][0]
        r = _dot(jnp.concatenate(_split3(lfd), axis=0), suf_ref[...])
        within = r[0:rows, 0:LANES] + r[rows:2 * rows, 0:LANES] + r[2 * rows:3 * rows, 0:LANES]
        rowtot = r[0:rows, LANES:] + r[rows:2 * rows, LANES:] + r[2 * rows:3 * rows, LANES:]
        incl = rowtot
        sft = 1
        while sft < rows:
            incl = incl + jnp.where(row_i + sft < rows, pltpu.roll(incl, rows - sft, 0), 0.0)
            sft *= 2
        bias_d = within + (incl - rowtot) + run
        run = run + incl[0:1, :]
        bias = jnp.concatenate([jnp.broadcast_to(bias_d[t:t + 1, :], (nh, LANES)) for t in range(rows)], axis=1)
        k2 = k_refs[i][0, 0].reshape(page * nh, FOX_HEAD_DIM).astype(BF16)
        scores.append(jnp.where(keep, _dot_nt(q, k2) * scale + bias, -jnp.inf))
    carry[...] = run

    m_old = m_scr[...]
    m_new = m_old
    for s in scores:
        m_new = jnp.maximum(m_new, jnp.max(s, axis=-1, keepdims=True))
    alpha = jnp.exp(m_old - m_new)
    l = alpha * l_scr[...]
    acc = alpha * acc_scr[...]
    for i, s in enumerate(scores):
        p = jnp.exp(s - m_new)
        l = l + jnp.sum(p, axis=-1, keepdims=True)
        v2 = v_refs[i][0, 0].reshape(page * nh, FOX_HEAD_DIM).astype(BF16)
        acc = acc + _dot(p.astype(BF16), v2)
    m_scr[...] = m_new
    l_scr[...] = l
    acc_scr[...] = acc

    @pl.when(j == n_steps - 1)
    def _():
        s_self = jnp.sum(q_ref[0] * kn_ref[0], axis=-1, keepdims=True) * scale
        m_old = m_scr[...]
        m_new = jnp.maximum(m_old, s_self)
        alpha = jnp.exp(m_old - m_new)
        p_self = jnp.exp(s_self - m_new)
        l = alpha * l_scr[...] + p_self
        acc = alpha * acc_scr[...] + p_self * vn_ref[0]
        o_ref[0] = (acc / l).astype(o_ref.dtype)


def fox_sample(page_table, q, k_new, v_new, lf_new_tiled, k_pool, v_pool, lf_pool_dense):
    nb, n_pages = page_table.shape
    page = k_pool.shape[2]
    rows = lf_pool_dense.shape[1]
    pp = PAGES_PER_STEP
    n_steps = n_pages // pp
    r_i = lax.broadcasted_iota(jnp.int32, (LANES, 2 * LANES), 0)
    c_i = lax.broadcasted_iota(jnp.int32, (LANES, 2 * LANES), 1)
    same = r_i % FOX_HEADS == c_i % FOX_HEADS
    suf = (same & ((c_i >= LANES) | (r_i > c_i))).astype(BF16)

    def newest_first(i):
        return lambda b, j, pt: pt[b, n_pages - 1 - (j * pp + i)]

    def kv_spec(i):
        pick = newest_first(i)
        return pl.BlockSpec((1, 1, page, FOX_HEADS, FOX_HEAD_DIM), lambda b, j, pt: (0, pick(b, j, pt), 0, 0, 0))

    def lf_spec(i):
        pick = newest_first(i)
        return pl.BlockSpec((1, rows, LANES), lambda b, j, pt: (pick(b, j, pt), 0, 0))

    per_b = lambda shape: pl.BlockSpec((1,) + shape, lambda b, j, pt: (b, 0, 0))
    hd = (FOX_HEADS, FOX_HEAD_DIM)
    in_specs = [per_b(hd), per_b(hd), per_b(hd), per_b((1, LANES)),
                pl.BlockSpec((LANES, 2 * LANES), lambda b, j, pt: (0, 0))]
    in_specs += [kv_spec(i) for i in range(pp)]
    in_specs += [kv_spec(i) for i in range(pp)]
    in_specs += [lf_spec(i) for i in range(pp)]
    grid_spec = pltpu.PrefetchScalarGridSpec(
        num_scalar_prefetch=1,
        grid=(nb, n_steps),
        in_specs=in_specs,
        out_specs=per_b(hd),
        scratch_shapes=[
            pltpu.VMEM((FOX_HEADS, 1), F32),
            pltpu.VMEM((FOX_HEADS, 1), F32),
            pltpu.VMEM(hd, F32),
            pltpu.VMEM((1, LANES), F32),
        ],
    )
    return pl.pallas_call(
        functools.partial(_fox_s_kernel, n_steps=n_steps, scale=FOX_HEAD_DIM ** -0.5),
        grid_spec=grid_spec,
        out_shape=jax.ShapeDtypeStruct((nb,) + hd, BF16),
        compiler_params=_cparams("parallel", "arbitrary"),
        name="fox_sample",
    )(page_table, q.reshape((nb,) + hd), k_new.reshape((nb,) + hd), v_new.reshape((nb,) + hd),
      lf_new_tiled.reshape(nb, 1, LANES), suf,
      *([k_pool] * pp), *([v_pool] * pp), *([lf_pool_dense] * pp))


def _mem_s_kernel(q_ref, k_ref, v_ref, o_ref, *, scale):
    n_mem = k_ref.shape[2]
    k2 = k_ref[0, 0].reshape(n_mem * MEM_HEADS, MEM_HEAD_DIM).astype(BF16)
    v2 = v_ref[0, 0].reshape(n_mem * MEM_HEADS, MEM_HEAD_DIM).astype(BF16)
    q = q_ref[0]
    keep = _head_matched((q.shape[0], n_mem * MEM_HEADS), MEM_HEADS)
    s = jnp.where(keep, _dot_nt(q, k2) * scale, -jnp.inf)
    e = jnp.exp(s - jnp.max(s, axis=-1, keepdims=True))
    p = e / jnp.sum(e, axis=-1, keepdims=True)
    o_ref[0] = _dot(p.astype(BF16), v2).astype(o_ref.dtype)


def mem_sample_attend(q_rep, k, v):
    _, nb, n_mem, _, _ = k.shape
    rep = q_rep.shape[1]
    kv = pl.BlockSpec((1, 1, n_mem, MEM_HEADS, MEM_HEAD_DIM), lambda b: (0, b, 0, 0, 0))
    return pl.pallas_call(
        functools.partial(_mem_s_kernel, scale=MEM_HEAD_DIM ** -0.5),
        grid=(nb,),
        in_specs=[pl.BlockSpec((1, rep, MEM_HEAD_DIM), lambda b: (b, 0, 0)), kv, kv],
        out_specs=pl.BlockSpec((1, rep, MEM_HEAD_DIM), lambda b: (b, 0, 0)),
        out_shape=jax.ShapeDtypeStruct((nb, rep, MEM_HEAD_DIM), BF16),
        compiler_params=_cparams("parallel"),
        name="mem_sample_attend",
    )(q_rep, k, v)


def _ffn_s_kernel(g_ref, u_ref, buf_ref, cw_ref, cb_ref, o_ref):
    cw = cw_ref[...]
    conv = buf_ref[0] * cw[0:1, :]
    conv = conv + buf_ref[1] * cw[1:2, :]
    conv = conv + g_ref[...] * cw[2:3, :]
    conv = conv + cb_ref[...]
    o_ref[...] = (_silu(conv) * u_ref[...]).astype(o_ref.dtype)


def ffn_sample_act(g, u, buf_t, conv_w, conv_b):
    full = lambda s: pl.BlockSpec(s, lambda i: (0,) * len(s))
    return pl.pallas_call(
        _ffn_s_kernel,
        grid=(1,),
        in_specs=[full(g.shape), full(u.shape), full(buf_t.shape), full(conv_w.shape), full(conv_b.shape)],
        out_specs=full(g.shape),
        out_shape=jax.ShapeDtypeStruct(g.shape, BF16),
        compiler_params=_cparams("arbitrary"),
        name="ffn_sample_act",
    )(g, u, buf_t, conv_w, conv_b)


def _project(xn, w, *, tm):
    mm = functools.partial(matmul, xn, tm=tm)
    (zs,) = mm(w["z"], tn=512, out_dtypes=(F32,), epilogue=lambda acc: (_silu(acc),), name="proj_z")
    (xbc,) = mm(w["xbc"], tn=512, out_dtypes=(F32,), name="proj_xbc")
    (dtf,) = mm(w["dtf"], tn=DTF_WIDTH, out_dtypes=(F32,), epilogue=_dtf_epilogue, rows=(w["dtf_bias"],),
                name="proj_dtf")
    (q,) = mm(w["q"], tn=512, out_dtypes=(BF16,), name="proj_q")
    k, k_bf = mm(w["k"], tn=512, out_dtypes=(F32, BF16), epilogue=lambda acc: (acc, acc), name="proj_k")
    v, v_bf = mm(w["v"], tn=512, out_dtypes=(F32, BF16), epilogue=lambda acc: (acc, acc), name="proj_v")
    (mq,) = mm(w["mq"], tn=512, out_dtypes=(BF16,), name="proj_mq")
    (gates,) = mm(w["gates"], tn=512, out_dtypes=(F32,), epilogue=lambda acc: (jax.nn.sigmoid(acc),),
                  name="proj_gates")
    return dict(zs=zs, xbc=xbc, dtf=dtf, q=q, k=k, k_bf=k_bf, v=v, v_bf=v_bf, mq=mq, gates=gates)


def kernel(x_prompt, x_sample, cache_fox_k, cache_fox_v, cache_fox_logf, cache_mem_k, cache_mem_v, state_ssd,
           state_ssd_conv, state_ffn_conv, page_table, mem_prompt, g_mix, w_in, ssd_conv_w, ssd_conv_b,
           ssd_dt_bias, ssd_a_log, ssd_d, ssd_g_norm, fox_b_forget, g_mem, w_mem_kv, w_branch, w_out, g_ffn,
           w_ffn_gate, w_ffn_up, ffn_conv_w, ffn_conv_b, w_ffn_down, g_final):
    depth = w_in.shape[0]
    assert depth == 1, "single-layer trunk"
    bp, seq, d = x_prompt.shape
    nb, dec_seq, _ = x_sample.shape
    assert d == D_MODEL and dec_seq == 1 and seq % SSD_CHUNK == 0
    n_mem = mem_prompt.shape[1]
    n_pool, page = cache_fox_k.shape[1], cache_fox_k.shape[2]
    assert page_table.shape[1] % PAGES_PER_STEP == 0
    mp = bp * seq

    wi = w_in[0]
    offs = [0]
    for width in (SSD_INNER, SSD_CONV_DIM, SSD_HEADS, FOX_WIDTH, FOX_WIDTH, FOX_WIDTH, FOX_HEADS, MEM_WIDTH,
                  N_BRANCH * D_MODEL):
        offs.append(offs[-1] + width)
    assert offs[-1] == wi.shape[1]
    seg = lambda i: wi[:, offs[i]:offs[i + 1]].astype(BF16)
    dtf_pad = DTF_WIDTH - SSD_HEADS - FOX_HEADS
    w = dict(
        z=seg(0), xbc=seg(1), q=seg(3), k=seg(4), v=seg(5), mq=seg(7), gates=seg(8),
        dtf=jnp.concatenate([seg(2), seg(6), jnp.zeros((D_MODEL, dtf_pad), BF16)], axis=1),
        dtf_bias=jnp.concatenate([ssd_dt_bias[0], fox_b_forget[0], jnp.zeros((dtf_pad,), F32)]).reshape(1, DTF_WIDTH),
    )
    wb = w_branch[0]
    w_s = wb[:SSD_INNER].astype(BF16)
    w_f = wb[SSD_INNER:SSD_INNER + FOX_WIDTH].astype(BF16)
    w_m = wb[SSD_INNER + FOX_WIDTH:].astype(BF16)
    w_o = w_out[0].astype(BF16)
    w_g = w_ffn_gate[0].astype(BF16)
    w_u = w_ffn_up[0].astype(BF16)
    w_d = w_ffn_down[0].astype(BF16)
    w_mk = w_mem_kv[0][:, :MEM_WIDTH].astype(BF16)
    w_mv = w_mem_kv[0][:, MEM_WIDTH:].astype(BF16)
    a_row = jnp.concatenate([-jnp.exp(ssd_a_log[0]), jnp.zeros((DTF_WIDTH - SSD_HEADS,), F32)]).reshape(1, DTF_WIDTH)
    dskip_row = jnp.repeat(ssd_d[0], SSD_HEAD_DIM).reshape(1, SSD_INNER)
    gnorm_row = ssd_g_norm[0].reshape(1, SSD_INNER)
    conv_w = ssd_conv_w[0]
    conv_b = ssd_conv_b[0].reshape(1, SSD_CONV_DIM)
    fconv_w = ffn_conv_w[0]
    fconv_b = ffn_conv_b[0].reshape(1, D_FF)

    xp = x_prompt.reshape(mp, D_MODEL)
    pr = _project(rms_norm(xp, g_mix[0], tm=512), w, tm=1024)

    mem_n = rms_norm(mem_prompt.reshape(bp * n_mem, D_MODEL), g_mem[0], tm=512)
    mk_p, mk_bf = matmul(mem_n, w_mk, tm=1024, tn=512, out_dtypes=(F32, BF16), epilogue=lambda acc: (acc, acc),
                         name="proj_mem_k")
    mv_p, mv_bf = matmul(mem_n, w_mv, tm=1024, tn=512, out_dtypes=(F32, BF16), epilogue=lambda acc: (acc, acc),
                         name="proj_mem_v")

    o_ssd_p, st_p = ssd_prompt(pr["xbc"], pr["dtf"], pr["zs"], conv_w, conv_b, a_row, dskip_row, gnorm_row,
                               batch=bp, seq=seq)

    tq, tk = 512, 512
    c_all = seq_cumsum(pr["dtf"], seq=seq)
    c_p = c_all[:, SSD_HEADS:SSD_HEADS + FOX_HEADS].reshape(bp, seq, FOX_HEADS).transpose(0, 2, 1)
    o_fox_p = fox_prompt(pr["q"], pr["k_bf"], pr["v_bf"], c_p.reshape(bp, FOX_HEADS, seq, 1),
                         c_p.reshape(bp, FOX_HEADS, seq // tk, 1, tk), batch=bp, seq=seq, tq=tq, tk=tk, heads=2)
    o_mem_p = mem_prompt_attend(pr["mq"], mk_bf, mv_bf, batch=bp, seq=seq, n_mem=n_mem, tq=512)

    merged_p = branch_merge(o_ssd_p, o_fox_p, o_mem_p, w_s, w_f, w_m, pr["gates"], tm=512, tn=512)
    h_p, hn_p = out_proj(merged_p, w_o, xp, g_ffn[0], tm=256)
    act_p, fbuf_p = ffn_gate_up(hn_p, w_g, w_u, fconv_w, fconv_b, batch=bp, seq=seq, tm=1024, tn=512)
    y_p = ffn_down(act_p, w_d, h_p, g_final, tm=512, tk=512)

    xs_ = x_sample.reshape(nb, D_MODEL)
    sm = _project(rms_norm(xs_, g_mix[0], tm=nb), w, tm=nb)

    buf_t = state_ssd_conv[0].transpose(1, 0, 2)
    h_i = lax.broadcasted_iota(jnp.int32, (DTF_WIDTH, SSD_INNER), 0)
    col_i = lax.broadcasted_iota(jnp.int32, (DTF_WIDTH, SSD_INNER), 1)
    expand = (col_i // SSD_HEAD_DIM == h_i).astype(BF16)
    xc_s, cols_s = ssd_sample_prep(sm["xbc"], buf_t, conv_w, conv_b, sm["dtf"], a_row, expand)
    st_s, o_ssd_s = ssd_sample_step(cols_s, state_ssd[0].reshape(nb, SSD_HEADS * SSD_HEAD_DIM, SSD_STATE), xc_s,
                                    sm["zs"], dskip_row, gnorm_row)

    lf_s = sm["dtf"][:, SSD_HEADS:SSD_HEADS + FOX_HEADS]
    lf_dense = cache_fox_logf[0].reshape(n_pool, page * FOX_HEADS // LANES, LANES)
    o_fox_s = fox_sample(page_table, sm["q"].astype(F32), sm["k"], sm["v"], jnp.tile(lf_s, (1, LANES // FOX_HEADS)),
                         cache_fox_k, cache_fox_v, lf_dense)
    mq_rep = jnp.tile(sm["mq"].reshape(nb, MEM_HEADS, MEM_HEAD_DIM), (1, 2 * SUBLANES // MEM_HEADS, 1))
    o_mem_s = mem_sample_attend(mq_rep, cache_mem_k, cache_mem_v)
    o_mem_s = o_mem_s[:, :MEM_HEADS].reshape(nb, MEM_WIDTH)

    merged_s = branch_merge(o_ssd_s.reshape(nb, SSD_INNER), o_fox_s.reshape(nb, FOX_WIDTH), o_mem_s, w_s, w_f, w_m,
                            sm["gates"], tm=nb, tn=512)
    h_s, hn_s = out_proj(merged_s, w_o, xs_, g_ffn[0], tm=nb)
    (g_s,) = matmul(hn_s, w_g, tm=nb, tn=512, out_dtypes=(F32,), name="ffn_gate_s")
    (u_s,) = matmul(hn_s, w_u, tm=nb, tn=512, out_dtypes=(F32,), name="ffn_up_s")
    act_s = ffn_sample_act(g_s, u_s, state_ffn_conv[0].transpose(1, 0, 2), fconv_w, fconv_b)
    y_s = ffn_down(act_s, w_d, h_s, g_final, tm=nb, tk=512)

    lf_p = pr["dtf"][:, SSD_HEADS:SSD_HEADS + FOX_HEADS]
    heads5 = lambda t, b, l: t.reshape(1, b, l, FOX_HEADS, FOX_HEAD_DIM)
    return (
        y_p.reshape(bp, seq, D_MODEL),
        y_s.reshape(nb, 1, D_MODEL),
        heads5(pr["k"], bp, seq),
        heads5(pr["v"], bp, seq),
        lf_p.reshape(1, bp, seq, FOX_HEADS),
        mk_p.reshape(1, bp, n_mem, MEM_HEADS, MEM_HEAD_DIM),
        mv_p.reshape(1, bp, n_mem, MEM_HEADS, MEM_HEAD_DIM),
        st_p.reshape(1, bp, SSD_HEADS, SSD_HEAD_DIM, SSD_STATE),
        pr["xbc"].reshape(bp, seq, SSD_CONV_DIM)[:, seq - (SSD_CONV - 1):][None],
        fbuf_p[None],
        heads5(sm["k"], nb, 1),
        heads5(sm["v"], nb, 1),
        lf_s.reshape(1, nb, 1, FOX_HEADS),
        st_s.reshape(1, nb, SSD_HEADS, SSD_HEAD_DIM, SSD_STATE),
        jnp.concatenate([state_ssd_conv[0][:, 1:], sm["xbc"][:, None, :]], axis=1)[None],
        jnp.concatenate([state_ffn_conv[0][:, 1:], g_s[:, None, :]], axis=1)[None],
    )
```

```python
import functools
import math

import jax
import jax.numpy as jnp
from jax import lax
from jax.experimental import pallas as pl
from jax.experimental.pallas import tpu as pltpu

F32 = jnp.float32
BF16 = jnp.bfloat16

D_MODEL = 2048
SSD_INNER = 4096
SSD_HEAD_DIM = 64
SSD_HEADS = 64
SSD_GROUPS = 8
SSD_STATE = 128
SSD_CONV = 4
SSD_CHUNK = 128
SSD_CONV_DIM = SSD_INNER + 2 * SSD_GROUPS * SSD_STATE
HEADS_PER_GROUP = SSD_HEADS // SSD_GROUPS
GROUP_WIDTH = HEADS_PER_GROUP * SSD_HEAD_DIM
FOX_HEAD_DIM = 128
FOX_HEADS = 16
FOX_WIDTH = FOX_HEADS * FOX_HEAD_DIM
MEM_HEADS = 4
MEM_HEAD_DIM = 128
MEM_WIDTH = MEM_HEADS * MEM_HEAD_DIM
D_FF = 5632
FFN_CONV = 3
N_BRANCH = 3
EPS = 1e-6

LANES = 128
SUBLANES = 8
VMEM_LIMIT_BYTES = 56 * 1024 * 1024

DTF_WIDTH = LANES
PAGES_PER_STEP = 8


def _cparams(*semantics):
    return pltpu.CompilerParams(dimension_semantics=semantics, vmem_limit_bytes=VMEM_LIMIT_BYTES)


def _dot(a, b):
    return jnp.dot(a, b, preferred_element_type=F32)


def _dot_nt(a, b):
    return lax.dot_general(a, b, (((1,), (1,)), ((), ())), preferred_element_type=F32)


def _silu(x):
    return x * jax.nn.sigmoid(x)


def _softplus(x):
    return jnp.maximum(x, 0.0) + jnp.log1p(jnp.exp(-jnp.abs(x)))


def _split3(x):
    hi = x.astype(BF16)
    r1 = x - hi.astype(F32)
    mid = r1.astype(BF16)
    lo = (r1 - mid.astype(F32)).astype(BF16)
    return hi, mid, lo


def _rms_kernel(x_ref, g_ref, o_ref):
    x = x_ref[...]
    ms = jnp.mean(x * x, axis=-1, keepdims=True)
    o_ref[...] = (x * lax.rsqrt(ms + EPS) * g_ref[...]).astype(o_ref.dtype)


def rms_norm(x, g, *, tm, out_dtype=BF16):
    m, d = x.shape
    return pl.pallas_call(
        _rms_kernel,
        grid=(m // tm,),
        in_specs=[pl.BlockSpec((tm, d), lambda i: (i, 0)), pl.BlockSpec((1, d), lambda i: (0, 0))],
        out_specs=pl.BlockSpec((tm, d), lambda i: (i, 0)),
        out_shape=jax.ShapeDtypeStruct((m, d), out_dtype),
        compiler_params=_cparams("parallel"),
        name="rms_norm",
    )(x, g.reshape(1, d))


def _mm_kernel(*refs, epilogue, n_rows):
    a_ref, w_ref = refs[0], refs[1]
    rows = [r[...] for r in refs[2:2 + n_rows]]
    outs = refs[2 + n_rows:]
    acc = _dot(a_ref[...], w_ref[...])
    vals = epilogue(acc, *rows)
    for o, v in zip(outs, vals):
        o[...] = v.astype(o.dtype)


def matmul(a, w, *, tm, tn, out_dtypes, epilogue=lambda acc: (acc,), rows=(), name="matmul"):
    m, k = a.shape
    n = w.shape[1]
    tm = min(tm, m)
    tn = min(tn, n)
    in_specs = [pl.BlockSpec((tm, k), lambda i, j: (i, 0)), pl.BlockSpec((k, tn), lambda i, j: (0, j))]
    in_specs += [pl.BlockSpec((1, tn), lambda i, j: (0, j)) for _ in rows]
    outs = pl.pallas_call(
        functools.partial(_mm_kernel, epilogue=epilogue, n_rows=len(rows)),
        grid=(m // tm, n // tn),
        in_specs=in_specs,
        out_specs=[pl.BlockSpec((tm, tn), lambda i, j: (i, j)) for _ in out_dtypes],
        out_shape=[jax.ShapeDtypeStruct((m, n), dt) for dt in out_dtypes],
        compiler_params=_cparams("parallel", "arbitrary"),
        name=name,
    )(a, w, *rows)
    return outs


def _dtf_epilogue(acc, bias):
    x = acc + bias
    lane = lax.broadcasted_iota(jnp.int32, x.shape, 1)
    return (jnp.where(lane < SSD_HEADS, _softplus(x), -_softplus(-x)),)


def _cumsum_kernel(x_ref, o_ref):
    x = x_ref[...]
    n = x.shape[0]
    row = lax.broadcasted_iota(jnp.int32, x.shape, 0)
    s = 1
    while s < n:
        x = x + jnp.where(row >= s, pltpu.roll(x, s, 0), 0.0)
        s *= 2
    o_ref[...] = x


def seq_cumsum(x, *, seq):
    m, d = x.shape
    return pl.pallas_call(
        _cumsum_kernel,
        grid=(m // seq,),
        in_specs=[pl.BlockSpec((seq, d), lambda b: (b, 0))],
        out_specs=pl.BlockSpec((seq, d), lambda b: (b, 0)),
        out_shape=jax.ShapeDtypeStruct((m, d), F32),
        compiler_params=_cparams("parallel"),
        name="seq_cumsum",
    )(x)


def _pair_cols(mat, pair, lane_lo):
    a = mat[:, 2 * pair:2 * pair + 1]
    b = mat[:, 2 * pair + 1:2 * pair + 2]
    return jnp.where(lane_lo, a, b)


def _ssd_kernel(xbc_ref, dtf_ref, zs_ref, cw_ref, cb_ref, a_ref, dskip_ref, gn_ref,
                o_ref, st_out_ref, xbuf, xc, st, y_scr, xw_scr):
    c = pl.program_id(1)
    nc = pl.num_programs(1)
    cs = SSD_CHUNK
    halo = SUBLANES

    @pl.when(c == 0)
    def _():
        xbuf[0:halo, :] = jnp.zeros((halo, SSD_CONV_DIM), F32)
        st[...] = jnp.zeros_like(st)

    @pl.when(c > 0)
    def _():
        xbuf[0:halo, :] = xbuf[cs:cs + halo, :]

    xbuf[halo:halo + cs, :] = xbc_ref[...]

    cw = cw_ref[...]
    cb = cb_ref[...]
    col_chunk = 512
    for j in range(SSD_CONV_DIM // col_chunk):
        sl = slice(j * col_chunk, (j + 1) * col_chunk)
        acc = xbuf[halo - 3:halo - 3 + cs, sl] * cw[0:1, sl]
        for i in range(1, SSD_CONV):
            acc = acc + xbuf[halo - 3 + i:halo - 3 + i + cs, sl] * cw[i:i + 1, sl]
        xc[:, sl] = _silu(acc + cb[:, sl])

    dt = dtf_ref[...]
    da = dt * a_ref[...]
    r_i = lax.broadcasted_iota(jnp.int32, (cs, cs), 0)
    c_i = lax.broadcasted_iota(jnp.int32, (cs, cs), 1)
    causal = c_i <= r_i
    tril = jnp.where(causal, 1.0, 0.0)
    a_cs = jnp.dot(tril, da, preferred_element_type=F32, precision=lax.Precision.HIGHEST)
    a_cs_t = a_cs.T
    dt_t = dt.T
    last = a_cs[cs - 1:cs, :]
    ecs = jnp.exp(a_cs)
    wfac = dt * jnp.exp(last - a_cs)
    cdec = jnp.exp(last)

    lane_lo = lax.broadcasted_iota(jnp.int32, (cs, LANES), 1) < SSD_HEAD_DIM
    lane_lo_row = lane_lo[0:1, :]
    pairs_per_group = HEADS_PER_GROUP // 2
    for g in range(SSD_GROUPS):
        b_off = SSD_INNER + g * SSD_STATE
        c_off = SSD_INNER + SSD_GROUPS * SSD_STATE + g * SSD_STATE
        bg = xc[:, b_off:b_off + SSD_STATE]
        cg = xc[:, c_off:c_off + SSD_STATE].astype(BF16)
        cbm = _dot_nt(cg, bg.astype(BF16))
        yoff = _dot(cg, st[g].astype(BF16))
        cd_tiles = []
        for pr in range(pairs_per_group):
            pair = g * pairs_per_group + pr
            col0 = pair * LANES
            xs_pair = xc[:, col0:col0 + LANES]
            xs_bf = xs_pair.astype(BF16)
            yd = []
            for h in (2 * pair, 2 * pair + 1):
                seg = a_cs[:, h:h + 1] - a_cs_t[h:h + 1, :]
                decay = jnp.exp(jnp.where(causal, seg, -jnp.inf))
                mh = (cbm * decay * dt_t[h:h + 1, :]).astype(BF16)
                yd.append(_dot(mh, xs_bf))
            y_pair = jnp.where(lane_lo, yd[0], yd[1])
            y_pair = y_pair + _pair_cols(ecs, pair, lane_lo) * yoff[:, pr * LANES:(pr + 1) * LANES]
            y_pair = y_pair + xs_pair * dskip_ref[:, col0:col0 + LANES]
            y_scr[:, col0:col0 + LANES] = y_pair
            xw_scr[:, pr * LANES:(pr + 1) * LANES] = (xs_pair * _pair_cols(wfac, pair, lane_lo)).astype(BF16)
            cd_tiles.append(_pair_cols(cdec, pair, lane_lo_row))
        cd_row = jnp.concatenate(cd_tiles, axis=1)
        st[g] = st[g] * cd_row + _dot(bg.T.astype(BF16), xw_scr[...])

    v = y_scr[...] * zs_ref[...]
    ms = jnp.mean(v * v, axis=-1, keepdims=True)
    o_ref[...] = (v * lax.rsqrt(ms + EPS) * gn_ref[...]).astype(o_ref.dtype)

    @pl.when(c == nc - 1)
    def _():
        for g in range(SSD_GROUPS):
            st_out_ref[0, g * GROUP_WIDTH:(g + 1) * GROUP_WIDTH, :] = st[g].T


def ssd_prompt(xbc, dtf, zs, conv_w, conv_b, a_row, dskip_row, gnorm, *, batch, seq):
    nc = seq // SSD_CHUNK
    cs = SSD_CHUNK
    row = lambda b, c: (b * nc + c, 0)
    const = lambda b, c: (0, 0)
    return pl.pallas_call(
        _ssd_kernel,
        grid=(batch, nc),
        in_specs=[
            pl.BlockSpec((cs, SSD_CONV_DIM), row),
            pl.BlockSpec((cs, DTF_WIDTH), row),
            pl.BlockSpec((cs, SSD_INNER), row),
            pl.BlockSpec((SSD_CONV, SSD_CONV_DIM), const),
            pl.BlockSpec((1, SSD_CONV_DIM), const),
            pl.BlockSpec((1, DTF_WIDTH), const),
            pl.BlockSpec((1, SSD_INNER), const),
            pl.BlockSpec((1, SSD_INNER), const),
        ],
        out_specs=[
            pl.BlockSpec((cs, SSD_INNER), row),
            pl.BlockSpec((1, SSD_HEADS * SSD_HEAD_DIM, SSD_STATE), lambda b, c: (b, 0, 0)),
        ],
        out_shape=[
            jax.ShapeDtypeStruct((batch * seq, SSD_INNER), BF16),
            jax.ShapeDtypeStruct((batch, SSD_HEADS * SSD_HEAD_DIM, SSD_STATE), F32),
        ],
        scratch_shapes=[
            pltpu.VMEM((cs + SUBLANES, SSD_CONV_DIM), F32),
            pltpu.VMEM((cs, SSD_CONV_DIM), F32),
            pltpu.VMEM((SSD_GROUPS, SSD_STATE, GROUP_WIDTH), F32),
            pltpu.VMEM((cs, SSD_INNER), F32),
            pltpu.VMEM((cs, GROUP_WIDTH), BF16),
        ],
        compiler_params=_cparams("parallel", "arbitrary"),
        name="ssd_prompt",
    )(xbc, dtf, zs, conv_w, conv_b, a_row, dskip_row, gnorm)


def _fox_kernel(q_ref, k_ref, v_ref, cq_ref, ck_ref, o_ref, *, tq, tk, heads, scale):
    qi = pl.program_id(2)
    dh = FOX_HEAD_DIM
    ratio = tq // tk
    qs = [q_ref[:, h * dh:(h + 1) * dh] for h in range(heads)]
    cqs = [cq_ref[0, h] for h in range(heads)]

    def step(j, carry, masked):
        off = pl.multiple_of(j * tk, tk)
        kk = k_ref[pl.ds(off, tk), :]
        vv = v_ref[pl.ds(off, tk), :]
        if masked:
            r_i = lax.broadcasted_iota(jnp.int32, (tq, tk), 0) + qi * tq
            c_i = lax.broadcasted_iota(jnp.int32, (tq, tk), 1) + j * tk
            visible = c_i <= r_i
        out = []
        for h in range(heads):
            m, l, acc = carry[3 * h:3 * h + 3]
            s = _dot_nt(qs[h], kk[:, h * dh:(h + 1) * dh]) * scale
            s = s + cqs[h] - ck_ref[0, h, j]
            if masked:
                s = jnp.where(visible, s, -jnp.inf)
            m_new = jnp.maximum(m, jnp.max(s, axis=-1, keepdims=True))
            p = jnp.exp(s - m_new)
            alpha = jnp.exp(m - m_new)
            l = alpha * l + jnp.sum(p, axis=-1, keepdims=True)
            acc = alpha * acc + _dot(p.astype(BF16), vv[:, h * dh:(h + 1) * dh])
            out += [m_new, l, acc]
        return tuple(out)

    init = (jnp.full((tq, 1), -jnp.inf, F32), jnp.zeros((tq, 1), F32), jnp.zeros((tq, dh), F32)) * heads
    carry = lax.fori_loop(0, qi * ratio, lambda j, cr: step(j, cr, False), init)
    for jj in range(ratio):
        carry = step(qi * ratio + jj, carry, True)
    for h in range(heads):
        _, l, acc = carry[3 * h:3 * h + 3]
        o_ref[:, h * dh:(h + 1) * dh] = (acc / l).astype(o_ref.dtype)


def fox_prompt(q, k, v, cq, ck, *, batch, seq, tq, tk, heads):
    nq = seq // tq
    width = heads * FOX_HEAD_DIM
    return pl.pallas_call(
        functools.partial(_fox_kernel, tq=tq, tk=tk, heads=heads, scale=FOX_HEAD_DIM ** -0.5),
        grid=(batch, FOX_HEADS // heads, nq),
        in_specs=[
            pl.BlockSpec((tq, width), lambda b, h, i: (b * nq + i, h)),
            pl.BlockSpec((seq, width), lambda b, h, i: (b, h)),
            pl.BlockSpec((seq, width), lambda b, h, i: (b, h)),
            pl.BlockSpec((1, heads, tq, 1), lambda b, h, i: (b, h, i, 0)),
            pl.BlockSpec((1, heads, seq // tk, 1, tk), lambda b, h, i: (b, h, 0, 0, 0)),
        ],
        out_specs=pl.BlockSpec((tq, width), lambda b, h, i: (b * nq + i, h)),
        out_shape=jax.ShapeDtypeStruct((batch * seq, FOX_WIDTH), BF16),
        compiler_params=_cparams("parallel", "parallel", "arbitrary"),
        name="fox_prompt",
    )(q, k, v, cq, ck)


def _mem_kernel(q_ref, k_ref, v_ref, o_ref, *, scale):
    dh = MEM_HEAD_DIM
    for h in range(MEM_HEADS):
        cols = slice(h * dh, (h + 1) * dh)
        s = _dot_nt(q_ref[:, cols], k_ref[:, cols]) * scale
        e = jnp.exp(s - jnp.max(s, axis=-1, keepdims=True))
        p = e / jnp.sum(e, axis=-1, keepdims=True)
        o_ref[:, cols] = _dot(p.astype(BF16), v_ref[:, cols]).astype(o_ref.dtype)


def mem_prompt_attend(q, k, v, *, batch, seq, n_mem, tq):
    nq = seq // tq
    return pl.pallas_call(
        functools.partial(_mem_kernel, scale=MEM_HEAD_DIM ** -0.5),
        grid=(batch, nq),
        in_specs=[
            pl.BlockSpec((tq, MEM_WIDTH), lambda b, i: (b * nq + i, 0)),
            pl.BlockSpec((n_mem, MEM_WIDTH), lambda b, i: (b, 0)),
            pl.BlockSpec((n_mem, MEM_WIDTH), lambda b, i: (b, 0)),
        ],
        out_specs=pl.BlockSpec((tq, MEM_WIDTH), lambda b, i: (b * nq + i, 0)),
        out_shape=jax.ShapeDtypeStruct((batch * seq, MEM_WIDTH), BF16),
        compiler_params=_cparams("parallel", "arbitrary"),
        name="mem_prompt_attend",
    )(q, k, v)


def _merge_kernel(os_ref, of_ref, om_ref, ws_ref, wf_ref, wm_ref, g0_ref, g1_ref, g2_ref, o_ref):
    u = g0_ref[...] * _dot(os_ref[...], ws_ref[...])
    u = u + g1_ref[...] * _dot(of_ref[...], wf_ref[...])
    u = u + g2_ref[...] * _dot(om_ref[...], wm_ref[...])
    o_ref[...] = u.astype(o_ref.dtype)


def branch_merge(o_ssd, o_fox, o_mem, w_s, w_f, w_m, gates, *, tm, tn):
    m = o_ssd.shape[0]
    tm = min(tm, m)
    nj = D_MODEL // tn
    act = lambda k: pl.BlockSpec((tm, k), lambda i, j: (i, 0))
    wgt = lambda k: pl.BlockSpec((k, tn), lambda i, j: (0, j))
    gate = lambda br: pl.BlockSpec((tm, tn), lambda i, j: (i, br * nj + j))
    return pl.pallas_call(
        _merge_kernel,
        grid=(m // tm, nj),
        in_specs=[act(SSD_INNER), act(FOX_WIDTH), act(MEM_WIDTH), wgt(SSD_INNER), wgt(FOX_WIDTH), wgt(MEM_WIDTH),
                  gate(0), gate(1), gate(2)],
        out_specs=pl.BlockSpec((tm, tn), lambda i, j: (i, j)),
        out_shape=jax.ShapeDtypeStruct((m, D_MODEL), BF16),
        compiler_params=_cparams("parallel", "arbitrary"),
        name="branch_merge",
    )(o_ssd, o_fox, o_mem, w_s, w_f, w_m, gates, gates, gates)


def _outproj_kernel(a_ref, w_ref, x_ref, g_ref, h_ref, hn_ref):
    h = x_ref[...] + _dot(a_ref[...], w_ref[...])
    h_ref[...] = h
    ms = jnp.mean(h * h, axis=-1, keepdims=True)
    hn_ref[...] = (h * lax.rsqrt(ms + EPS) * g_ref[...]).astype(hn_ref.dtype)


def out_proj(merged, w_out, x, g_ffn, *, tm):
    m = x.shape[0]
    tm = min(tm, m)
    row = lambda i: (i, 0)
    const = lambda i: (0, 0)
    return pl.pallas_call(
        _outproj_kernel,
        grid=(m // tm,),
        in_specs=[pl.BlockSpec((tm, D_MODEL), row), pl.BlockSpec((D_MODEL, D_MODEL), const),
                  pl.BlockSpec((tm, D_MODEL), row), pl.BlockSpec((1, D_MODEL), const)],
        out_specs=[pl.BlockSpec((tm, D_MODEL), row), pl.BlockSpec((tm, D_MODEL), row)],
        out_shape=[jax.ShapeDtypeStruct((m, D_MODEL), F32), jax.ShapeDtypeStruct((m, D_MODEL), BF16)],
        compiler_params=_cparams("parallel"),
        name="out_proj",
    )(merged, w_out, x, g_ffn.reshape(1, D_MODEL))


def _ffn_kernel(hn_ref, wg_ref, wu_ref, cw_ref, cb_ref, act_ref, buf_ref, gbuf, *, tiles_per_seq):
    i = pl.program_id(1)
    tm = hn_ref.shape[0]
    halo = SUBLANES

    @pl.when(i % tiles_per_seq == 0)
    def _():
        gbuf[0:halo, :] = jnp.zeros((halo, gbuf.shape[1]), F32)

    @pl.when(i % tiles_per_seq != 0)
    def _():
        gbuf[0:halo, :] = gbuf[tm:tm + halo, :]

    hn = hn_ref[...]
    gbuf[halo:halo + tm, :] = _dot(hn, wg_ref[...])
    cw = cw_ref[...]
    conv = gbuf[halo - 2:halo - 2 + tm, :] * cw[0:1, :]
    conv = conv + gbuf[halo - 1:halo - 1 + tm, :] * cw[1:2, :]
    conv = conv + gbuf[halo:halo + tm, :] * cw[2:3, :]
    conv = conv + cb_ref[...]
    act_ref[...] = (_silu(conv) * _dot(hn, wu_ref[...])).astype(act_ref.dtype)
    buf_ref[0] = gbuf[halo + tm - 2:halo + tm, :]


def ffn_gate_up(hn, w_gate, w_up, conv_w, conv_b, *, batch, seq, tm, tn):
    m = hn.shape[0]
    tiles_per_seq = seq // tm
    return pl.pallas_call(
        functools.partial(_ffn_kernel, tiles_per_seq=tiles_per_seq),
        grid=(D_FF // tn, m // tm),
        in_specs=[
            pl.BlockSpec((tm, D_MODEL), lambda j, i: (i, 0)),
            pl.BlockSpec((D_MODEL, tn), lambda j, i: (0, j)),
            pl.BlockSpec((D_MODEL, tn), lambda j, i: (0, j)),
            pl.BlockSpec((FFN_CONV, tn), lambda j, i: (0, j)),
            pl.BlockSpec((1, tn), lambda j, i: (0, j)),
        ],
        out_specs=[
            pl.BlockSpec((tm, tn), lambda j, i: (i, j)),
            pl.BlockSpec((1, FFN_CONV - 1, tn), lambda j, i: (i // tiles_per_seq, 0, j)),
        ],
        out_shape=[
            jax.ShapeDtypeStruct((m, D_FF), BF16),
            jax.ShapeDtypeStruct((batch, FFN_CONV - 1, D_FF), F32),
        ],
        scratch_shapes=[pltpu.VMEM((tm + SUBLANES, tn), F32)],
        compiler_params=_cparams("parallel", "arbitrary"),
        name="ffn_gate_up",
    )(hn, w_gate, w_up, conv_w, conv_b)


def _down_kernel(a_ref, w_ref, h_ref, g_ref, y_ref, acc):
    k = pl.program_id(1)

    @pl.when(k == 0)
    def _():
        acc[...] = h_ref[...]

    acc[...] += _dot(a_ref[...], w_ref[...])

    @pl.when(k == pl.num_programs(1) - 1)
    def _():
        h = acc[...]
        ms = jnp.mean(h * h, axis=-1, keepdims=True)
        y_ref[...] = h * lax.rsqrt(ms + EPS) * g_ref[...]


def ffn_down(act, w_down, h, g_final, *, tm, tk):
    m = h.shape[0]
    tm = min(tm, m)
    return pl.pallas_call(
        _down_kernel,
        grid=(m // tm, D_FF // tk),
        in_specs=[pl.BlockSpec((tm, tk), lambda i, k: (i, k)), pl.BlockSpec((tk, D_MODEL), lambda i, k: (k, 0)),
                  pl.BlockSpec((tm, D_MODEL), lambda i, k: (i, 0)), pl.BlockSpec((1, D_MODEL), lambda i, k: (0, 0))],
        out_specs=pl.BlockSpec((tm, D_MODEL), lambda i, k: (i, 0)),
        out_shape=jax.ShapeDtypeStruct((m, D_MODEL), F32),
        scratch_shapes=[pltpu.VMEM((tm, D_MODEL), F32)],
        compiler_params=_cparams("parallel", "arbitrary"),
        name="ffn_down",
    )(act, w_down, h, g_final.reshape(1, D_MODEL))


def _ssd_s_prep_kernel(xbc_ref, buf_ref, cw_ref, cb_ref, dtf_ref, a_ref, expand_ref, xc_ref, cols_ref):
    cw = cw_ref[...]
    acc = buf_ref[0] * cw[0:1, :]
    acc = acc + buf_ref[1] * cw[1:2, :]
    acc = acc + buf_ref[2] * cw[2:3, :]
    acc = acc + xbc_ref[...] * cw[3:4, :]
    xcv = _silu(acc + cb_ref[...])
    xc_ref[...] = xcv
    dt = dtf_ref[...]
    decay = jnp.exp(dt * a_ref[...])
    nb = dt.shape[0]
    parts = _split3(dt) + _split3(decay)
    stacked = jnp.concatenate(parts, axis=0)
    ex = _dot(stacked, expand_ref[...])
    dt_exp = ex[0:nb] + ex[nb:2 * nb] + ex[2 * nb:3 * nb]
    dec_exp = ex[3 * nb:4 * nb] + ex[4 * nb:5 * nb] + ex[5 * nb:6 * nb]
    xdt = xcv[:, 0:SSD_INNER] * dt_exp
    pad = jnp.zeros((LANES - 2 * nb, SSD_INNER), F32)
    cols_ref[...] = jnp.concatenate([xdt, dec_exp, pad], axis=0).T


def ssd_sample_prep(xbc, buf_t, conv_w, conv_b, dtf, a_row, expand):
    nb = xbc.shape[0]
    full = lambda s: pl.BlockSpec(s, lambda i: (0,) * len(s))
    return pl.pallas_call(
        _ssd_s_prep_kernel,
        grid=(1,),
        in_specs=[full(xbc.shape), full(buf_t.shape), full(conv_w.shape), full(conv_b.shape), full(dtf.shape),
                  full(a_row.shape), full(expand.shape)],
        out_specs=[full((nb, SSD_CONV_DIM)), full((SSD_INNER, LANES))],
        out_shape=[jax.ShapeDtypeStruct((nb, SSD_CONV_DIM), F32), jax.ShapeDtypeStruct((SSD_INNER, LANES), F32)],
        compiler_params=_cparams("arbitrary"),
        name="ssd_sample_prep",
    )(xbc, buf_t, conv_w, conv_b, dtf, a_row, expand)


def _ssd_s_step_kernel(cols_ref, h0_ref, xc_ref, zs_ref, dskip_ref, gn_ref, hn_ref, o_ref, *, nb):
    b = pl.program_id(0)
    cols = cols_ref[...]
    hi, mid, lo = _split3(cols)
    lhs = jnp.concatenate([hi, mid, lo], axis=1)
    r_i = lax.broadcasted_iota(jnp.int32, (3 * LANES, 2 * LANES), 0) % LANES
    c_i = lax.broadcasted_iota(jnp.int32, (3 * LANES, 2 * LANES), 1)
    sel = jnp.where(r_i == jnp.where(c_i < LANES, b, nb + b), 1.0, 0.0).astype(BF16)
    picked = _dot(lhs, sel)
    xb = picked[:, 0:LANES]
    db = picked[:, LANES:2 * LANES]
    xc = xc_ref[0]
    y_parts = []
    for g in range(SSD_GROUPS):
        rows = slice(g * GROUP_WIDTH, (g + 1) * GROUP_WIDTH)
        b_row = xc[:, SSD_INNER + g * SSD_STATE:SSD_INNER + (g + 1) * SSD_STATE]
        c_off = SSD_INNER + SSD_GROUPS * SSD_STATE + g * SSD_STATE
        c_row = xc[:, c_off:c_off + SSD_STATE]
        hn = h0_ref[0, rows, :] * db[rows, :] + xb[rows, :] * b_row
        hn_ref[0, rows, :] = hn
        c8 = jnp.broadcast_to(c_row, (SUBLANES, SSD_STATE)).astype(BF16)
        y_parts.append(_dot_nt(c8, hn.astype(BF16))[0:1, :])
    xs = xc[:, 0:SSD_INNER]
    y = jnp.concatenate(y_parts, axis=1) + xs * dskip_ref[...]
    v = y * zs_ref[0]
    ms = jnp.mean(v * v, axis=-1, keepdims=True)
    o_ref[0] = (v * lax.rsqrt(ms + EPS) * gn_ref[...]).astype(o_ref.dtype)


def ssd_sample_step(cols, h0, xc, zs, dskip_row, gnorm):
    nb = h0.shape[0]
    rows = SSD_HEADS * SSD_HEAD_DIM
    const2 = lambda b: (0, 0)
    per_b = lambda b: (b, 0, 0)
    return pl.pallas_call(
        functools.partial(_ssd_s_step_kernel, nb=nb),
        grid=(nb,),
        in_specs=[
            pl.BlockSpec((SSD_INNER, LANES), const2),
            pl.BlockSpec((1, rows, SSD_STATE), per_b),
            pl.BlockSpec((1, 1, SSD_CONV_DIM), per_b),
            pl.BlockSpec((1, 1, SSD_INNER), per_b),
            pl.BlockSpec((1, SSD_INNER), const2),
            pl.BlockSpec((1, SSD_INNER), const2),
        ],
        out_specs=[pl.BlockSpec((1, rows, SSD_STATE), per_b), pl.BlockSpec((1, 1, SSD_INNER), per_b)],
        out_shape=[jax.ShapeDtypeStruct((nb, rows, SSD_STATE), F32), jax.ShapeDtypeStruct((nb, 1, SSD_INNER), BF16)],
        compiler_params=_cparams("parallel"),
        name="ssd_sample_step",
    )(cols, h0, xc.reshape(nb, 1, SSD_CONV_DIM), zs.reshape(nb, 1, SSD_INNER), dskip_row, gnorm)


def _head_matched(shape, n_heads):
    r_i = lax.broadcasted_iota(jnp.int32, shape, 0)
    c_i = lax.broadcasted_iota(jnp.int32, shape, 1)
    return c_i % n_heads == r_i % n_heads


def _fox_s_kernel(pt_ref, *refs, n_steps, scale):
    del pt_ref
    pp = PAGES_PER_STEP
    nh = FOX_HEADS
    q_ref, kn_ref, vn_ref, lfn_ref, suf_ref = refs[0:5]
    k_refs = refs[5:5 + pp]
    v_refs = refs[5 + pp:5 + 2 * pp]
    lf_refs = refs[5 + 2 * pp:5 + 3 * pp]
    o_ref = refs[5 + 3 * pp]
    m_scr, l_scr, acc_scr, carry = refs[6 + 3 * pp:]
    j = pl.program_id(1)

    @pl.when(j == 0)
    def _():
        m_scr[...] = jnp.full_like(m_scr, -jnp.inf)
        l_scr[...] = jnp.zeros_like(l_scr)
        acc_scr[...] = jnp.zeros_like(acc_scr)
        carry[...] = lfn_ref[0]

    q = q_ref[0].astype(BF16)
    page = k_refs[0].shape[2]
    rows = lf_refs[0].shape[1]
    keep = _head_matched((nh, page * nh), nh)
    row_i = lax.broadcasted_iota(jnp.int32, (rows, LANES), 0)

    run = carry[...]
    scores = []
    for i in range(pp):
        lfd = lf_refs[i][0]
        r = _dot(jnp.concatenate(_split3(lfd), axis=0), suf_ref[...])
        within = r[0:rows, 0:LANES] + r[rows:2 * rows, 0:LANES] + r[2 * rows:3 * rows, 0:LANES]
        rowtot = r[0:rows, LANES:] + r[rows:2 * rows, LANES:] + r[2 * rows:3 * rows, LANES:]
        incl = rowtot
        sft = 1
        while sft < rows:
            incl = incl + jnp.where(row_i + sft < rows, pltpu.roll(incl, rows - sft, 0), 0.0)
            sft *= 2
        bias_d = within + (incl - rowtot) + run
        run = run + incl[0:1, :]
        bias = jnp.concatenate([jnp.broadcast_to(bias_d[t:t + 1, :], (nh, LANES)) for t in range(rows)], axis=1)
        k2 = k_refs[i][0, 0].reshape(page * nh, FOX_HEAD_DIM).astype(BF16)
        scores.append(jnp.where(keep, _dot_nt(q, k2) * scale + bias, -jnp.inf))
    carry[...] = run

    m_old = m_scr[...]
    m_new = m_old
    for s in scores:
        m_new = jnp.maximum(m_new, jnp.max(s, axis=-1, keepdims=True))
    alpha = jnp.exp(m_old - m_new)
    l = alpha * l_scr[...]
    acc = alpha * acc_scr[...]
    for i, s in enumerate(scores):
        p = jnp.exp(s - m_new)
        l = l + jnp.sum(p, axis=-1, keepdims=True)
        v2 = v_refs[i][0, 0].reshape(page * nh, FOX_HEAD_DIM).astype(BF16)
        acc = acc + _dot(p.astype(BF16), v2)
    m_scr[...] = m_new
    l_scr[...] = l
    acc_scr[...] = acc

    @pl.when(j == n_steps - 1)
    def _():
        s_self = jnp.sum(q_ref[0] * kn_ref[0], axis=-1, keepdims=True) * scale
        m_old = m_scr[...]
        m_new = jnp.maximum(m_old, s_self)
        alpha = jnp.exp(m_old - m_new)
        p_self = jnp.exp(s_self - m_new)
        l = alpha * l_scr[...] + p_self
        acc = alpha * acc_scr[...] + p_self * vn_ref[0]
        o_ref[0] = (acc / l).astype(o_ref.dtype)


def fox_sample(page_table, q, k_new, v_new, lf_new_tiled, k_pool, v_pool, lf_pool_dense):
    nb, n_pages = page_table.shape
    page = k_pool.shape[2]
    rows = lf_pool_dense.shape[1]
    pp = PAGES_PER_STEP
    n_steps = n_pages // pp
    r_i = lax.broadcasted_iota(jnp.int32, (LANES, 2 * LANES), 0)
    c_i = lax.broadcasted_iota(jnp.int32, (LANES, 2 * LANES), 1)
    same = r_i % FOX_HEADS == c_i % FOX_HEADS
    suf = (same & ((c_i >= LANES) | (r_i > c_i))).astype(BF16)

    def newest_first(i):
        return lambda b, j, pt: pt[b, n_pages - 1 - (j * pp + i)]

    def kv_spec(i):
        pick = newest_first(i)
        return pl.BlockSpec((1, 1, page, FOX_HEADS, FOX_HEAD_DIM), lambda b, j, pt: (0, pick(b, j, pt), 0, 0, 0))

    def lf_spec(i):
        pick = newest_first(i)
        return pl.BlockSpec((1, rows, LANES), lambda b, j, pt: (pick(b, j, pt), 0, 0))

    per_b = lambda shape: pl.BlockSpec((1,) + shape, lambda b, j, pt: (b, 0, 0))
    hd = (FOX_HEADS, FOX_HEAD_DIM)
    in_specs = [per_b(hd), per_b(hd), per_b(hd), per_b((1, LANES)),
                pl.BlockSpec((LANES, 2 * LANES), lambda b, j, pt: (0, 0))]
    in_specs += [kv_spec(i) for i in range(pp)]
    in_specs += [kv_spec(i) for i in range(pp)]
    in_specs += [lf_spec(i) for i in range(pp)]
    grid_spec = pltpu.PrefetchScalarGridSpec(
        num_scalar_prefetch=1,
        grid=(nb, n_steps),
        in_specs=in_specs,
        out_specs=per_b(hd),
        scratch_shapes=[
            pltpu.VMEM((FOX_HEADS, 1), F32),
            pltpu.VMEM((FOX_HEADS, 1), F32),
            pltpu.VMEM(hd, F32),
            pltpu.VMEM((1, LANES), F32),
        ],
    )
    return pl.pallas_call(
        functools.partial(_fox_s_kernel, n_steps=n_steps, scale=FOX_HEAD_DIM ** -0.5),
        grid_spec=grid_spec,
        out_shape=jax.ShapeDtypeStruct((nb,) + hd, BF16),
        compiler_params=_cparams("parallel", "arbitrary"),
        name="fox_sample",
    )(page_table, q.reshape((nb,) + hd), k_new.reshape((nb,) + hd), v_new.reshape((nb,) + hd),
      lf_new_tiled.reshape(nb, 1, LANES), suf,
      *([k_pool] * pp), *([v_pool] * pp), *([lf_pool_dense] * pp))


def _mem_s_kernel(q_ref, k_ref, v_ref, o_ref, *, scale):
    n_mem = k_ref.shape[2]
    k2 = k_ref[0, 0].reshape(n_mem * MEM_HEADS, MEM_HEAD_DIM).astype(BF16)
    v2 = v_ref[0, 0].reshape(n_mem * MEM_HEADS, MEM_HEAD_DIM).astype(BF16)
    q = q_ref[0]
    keep = _head_matched((q.shape[0], n_mem * MEM_HEADS), MEM_HEADS)
    s = jnp.where(keep, _dot_nt(q, k2) * scale, -jnp.inf)
    e = jnp.exp(s - jnp.max(s, axis=-1, keepdims=True))
    p = e / jnp.sum(e, axis=-1, keepdims=True)
    o_ref[0] = _dot(p.astype(BF16), v2).astype(o_ref.dtype)


def mem_sample_attend(q_rep, k, v):
    _, nb, n_mem, _, _ = k.shape
    rep = q_rep.shape[1]
    kv = pl.BlockSpec((1, 1, n_mem, MEM_HEADS, MEM_HEAD_DIM), lambda b: (0, b, 0, 0, 0))
    return pl.pallas_call(
        functools.partial(_mem_s_kernel, scale=MEM_HEAD_DIM ** -0.5),
        grid=(nb,),
        in_specs=[pl.BlockSpec((1, rep, MEM_HEAD_DIM), lambda b: (b, 0, 0)), kv, kv],
        out_specs=pl.BlockSpec((1, rep, MEM_HEAD_DIM), lambda b: (b, 0, 0)),
        out_shape=jax.ShapeDtypeStruct((nb, rep, MEM_HEAD_DIM), BF16),
        compiler_params=_cparams("parallel"),
        name="mem_sample_attend",
    )(q_rep, k, v)


def _ffn_s_kernel(g_ref, u_ref, buf_ref, cw_ref, cb_ref, o_ref):
    cw = cw_ref[...]
    conv = buf_ref[0] * cw[0:1, :]
    conv = conv + buf_ref[1] * cw[1:2, :]
    conv = conv + g_ref[...] * cw[2:3, :]
    conv = conv + cb_ref[...]
    o_ref[...] = (_silu(conv) * u_ref[...]).astype(o_ref.dtype)


def ffn_sample_act(g, u, buf_t, conv_w, conv_b):
    full = lambda s: pl.BlockSpec(s, lambda i: (0,) * len(s))
    return pl.pallas_call(
        _ffn_s_kernel,
        grid=(1,),
        in_specs=[full(g.shape), full(u.shape), full(buf_t.shape), full(conv_w.shape), full(conv_b.shape)],
        out_specs=full(g.shape),
        out_shape=jax.ShapeDtypeStruct(g.shape, BF16),
        compiler_params=_cparams("arbitrary"),
        name="ffn_sample_act",
    )(g, u, buf_t, conv_w, conv_b)


def _project(xn, w, *, tm, tn):
    mm = functools.partial(matmul, xn, tm=tm)
    (zs,) = mm(w["z"], tn=tn, out_dtypes=(F32,), epilogue=lambda acc: (_silu(acc),), name="proj_z")
    (xbc,) = mm(w["xbc"], tn=tn, out_dtypes=(F32,), name="proj_xbc")
    (dtf,) = mm(w["dtf"], tn=DTF_WIDTH, out_dtypes=(F32,), epilogue=_dtf_epilogue, rows=(w["dtf_bias"],),
                name="proj_dtf")
    (q,) = mm(w["q"], tn=tn, out_dtypes=(BF16,), name="proj_q")
    k, k_bf = mm(w["k"], tn=tn, out_dtypes=(F32, BF16), epilogue=lambda acc: (acc, acc), name="proj_k")
    v, v_bf = mm(w["v"], tn=tn, out_dtypes=(F32, BF16), epilogue=lambda acc: (acc, acc), name="proj_v")
    (mq,) = mm(w["mq"], tn=tn, out_dtypes=(BF16,), name="proj_mq")
    (gates,) = mm(w["gates"], tn=tn, out_dtypes=(F32,), epilogue=lambda acc: (jax.nn.sigmoid(acc),),
                  name="proj_gates")
    return dict(zs=zs, xbc=xbc, dtf=dtf, q=q, k=k, k_bf=k_bf, v=v, v_bf=v_bf, mq=mq, gates=gates)


def kernel(x_prompt, x_sample, cache_fox_k, cache_fox_v, cache_fox_logf, cache_mem_k, cache_mem_v, state_ssd,
           state_ssd_conv, state_ffn_conv, page_table, mem_prompt, g_mix, w_in, ssd_conv_w, ssd_conv_b,
           ssd_dt_bias, ssd_a_log, ssd_d, ssd_g_norm, fox_b_forget, g_mem, w_mem_kv, w_branch, w_out, g_ffn,
           w_ffn_gate, w_ffn_up, ffn_conv_w, ffn_conv_b, w_ffn_down, g_final):
    depth = w_in.shape[0]
    assert depth == 1, "single-layer trunk"
    bp, seq, d = x_prompt.shape
    nb, dec_seq, _ = x_sample.shape
    assert d == D_MODEL and dec_seq == 1 and seq % SSD_CHUNK == 0
    n_mem = mem_prompt.shape[1]
    n_pool, page = cache_fox_k.shape[1], cache_fox_k.shape[2]
    assert page_table.shape[1] % PAGES_PER_STEP == 0
    mp = bp * seq

    wi = w_in[0]
    offs = [0]
    for width in (SSD_INNER, SSD_CONV_DIM, SSD_HEADS, FOX_WIDTH, FOX_WIDTH, FOX_WIDTH, FOX_HEADS, MEM_WIDTH,
                  N_BRANCH * D_MODEL):
        offs.append(offs[-1] + width)
    assert offs[-1] == wi.shape[1]
    seg = lambda i: wi[:, offs[i]:offs[i + 1]].astype(BF16)
    dtf_pad = DTF_WIDTH - SSD_HEADS - FOX_HEADS
    w = dict(
        z=seg(0), xbc=seg(1), q=seg(3), k=seg(4), v=seg(5), mq=seg(7), gates=seg(8),
        dtf=jnp.concatenate([seg(2), seg(6), jnp.zeros((D_MODEL, dtf_pad), BF16)], axis=1),
        dtf_bias=jnp.concatenate([ssd_dt_bias[0], fox_b_forget[0], jnp.zeros((dtf_pad,), F32)]).reshape(1, DTF_WIDTH),
    )
    wb = w_branch[0]
    w_s = wb[:SSD_INNER].astype(BF16)
    w_f = wb[SSD_INNER:SSD_INNER + FOX_WIDTH].astype(BF16)
    w_m = wb[SSD_INNER + FOX_WIDTH:].astype(BF16)
    w_o = w_out[0].astype(BF16)
    w_g = w_ffn_gate[0].astype(BF16)
    w_u = w_ffn_up[0].astype(BF16)
    w_d = w_ffn_down[0].astype(BF16)
    w_mk = w_mem_kv[0][:, :MEM_WIDTH].astype(BF16)
    w_mv = w_mem_kv[0][:, MEM_WIDTH:].astype(BF16)
    a_row = jnp.concatenate([-jnp.exp(ssd_a_log[0]), jnp.zeros((DTF_WIDTH - SSD_HEADS,), F32)]).reshape(1, DTF_WIDTH)
    dskip_row = jnp.repeat(ssd_d[0], SSD_HEAD_DIM).reshape(1, SSD_INNER)
    gnorm_row = ssd_g_norm[0].reshape(1, SSD_INNER)
    conv_w = ssd_conv_w[0]
    conv_b = ssd_conv_b[0].reshape(1, SSD_CONV_DIM)
    fconv_w = ffn_conv_w[0]
    fconv_b = ffn_conv_b[0].reshape(1, D_FF)

    xp = x_prompt.reshape(mp, D_MODEL)
    pr = _project(rms_norm(xp, g_mix[0], tm=512), w, tm=2048, tn=512)

    mem_n = rms_norm(mem_prompt.reshape(bp * n_mem, D_MODEL), g_mem[0], tm=512)
    mk_p, mk_bf = matmul(mem_n, w_mk, tm=1024, tn=512, out_dtypes=(F32, BF16), epilogue=lambda acc: (acc, acc),
                         name="proj_mem_k")
    mv_p, mv_bf = matmul(mem_n, w_mv, tm=1024, tn=512, out_dtypes=(F32, BF16), epilogue=lambda acc: (acc, acc),
                         name="proj_mem_v")

    o_ssd_p, st_p = ssd_prompt(pr["xbc"], pr["dtf"], pr["zs"], conv_w, conv_b, a_row, dskip_row, gnorm_row,
                               batch=bp, seq=seq)

    c_all = seq_cumsum(pr["dtf"], seq=seq)
    c_p = c_all[:, SSD_HEADS:SSD_HEADS + FOX_HEADS].reshape(bp, seq, FOX_HEADS).transpose(0, 2, 1)
    tq, tk = 512, 512
    o_fox_p = fox_prompt(pr["q"], pr["k_bf"], pr["v_bf"], c_p.reshape(bp, FOX_HEADS, seq, 1),
                         c_p.reshape(bp, FOX_HEADS, seq // tk, 1, tk), batch=bp, seq=seq, tq=tq, tk=tk, heads=2)
    o_mem_p = mem_prompt_attend(pr["mq"], mk_bf, mv_bf, batch=bp, seq=seq, n_mem=n_mem, tq=1024)

    merged_p = branch_merge(o_ssd_p, o_fox_p, o_mem_p, w_s, w_f, w_m, pr["gates"], tm=512, tn=512)
    h_p, hn_p = out_proj(merged_p, w_o, xp, g_ffn[0], tm=256)
    act_p, fbuf_p = ffn_gate_up(hn_p, w_g, w_u, fconv_w, fconv_b, batch=bp, seq=seq, tm=1024, tn=512)
    y_p = ffn_down(act_p, w_d, h_p, g_final, tm=512, tk=D_FF // 2)

    xs_ = x_sample.reshape(nb, D_MODEL)
    sm = _project(rms_norm(xs_, g_mix[0], tm=nb), w, tm=nb, tn=2048)

    buf_t = state_ssd_conv[0].transpose(1, 0, 2)
    h_i = lax.broadcasted_iota(jnp.int32, (DTF_WIDTH, SSD_INNER), 0)
    col_i = lax.broadcasted_iota(jnp.int32, (DTF_WIDTH, SSD_INNER), 1)
    expand = (col_i // SSD_HEAD_DIM == h_i).astype(BF16)
    xc_s, cols_s = ssd_sample_prep(sm["xbc"], buf_t, conv_w, conv_b, sm["dtf"], a_row, expand)
    st_s, o_ssd_s = ssd_sample_step(cols_s, state_ssd[0].reshape(nb, SSD_HEADS * SSD_HEAD_DIM, SSD_STATE), xc_s,
                                    sm["zs"], dskip_row, gnorm_row)

    lf_s = sm["dtf"][:, SSD_HEADS:SSD_HEADS + FOX_HEADS]
    lf_dense = cache_fox_logf[0].reshape(n_pool, page * FOX_HEADS // LANES, LANES)
    o_fox_s = fox_sample(page_table, sm["q"].astype(F32), sm["k"], sm["v"], jnp.tile(lf_s, (1, LANES // FOX_HEADS)),
                         cache_fox_k, cache_fox_v, lf_dense)
    mq_rep = jnp.tile(sm["mq"].reshape(nb, MEM_HEADS, MEM_HEAD_DIM), (1, 2 * SUBLANES // MEM_HEADS, 1))
    o_mem_s = mem_sample_attend(mq_rep, cache_mem_k, cache_mem_v)
    o_mem_s = o_mem_s[:, :MEM_HEADS].reshape(nb, MEM_WIDTH)

    merged_s = branch_merge(o_ssd_s.reshape(nb, SSD_INNER), o_fox_s.reshape(nb, FOX_WIDTH), o_mem_s, w_s, w_f, w_m,
                            sm["gates"], tm=nb, tn=1024)
    h_s, hn_s = out_proj(merged_s, w_o, xs_, g_ffn[0], tm=nb)
    (g_s,) = matmul(hn_s, w_g, tm=nb, tn=D_FF // 2, out_dtypes=(F32,), name="ffn_gate_s")
    (u_s,) = matmul(hn_s, w_u, tm=nb, tn=D_FF // 2, out_dtypes=(F32,), name="ffn_up_s")
    act_s = ffn_sample_act(g_s, u_s, state_ffn_conv[0].transpose(1, 0, 2), fconv_w, fconv_b)
    y_s = ffn_down(act_s, w_d, h_s, g_final, tm=nb, tk=D_FF // 2)

    lf_p = pr["dtf"][:, SSD_HEADS:SSD_HEADS + FOX_HEADS]
    heads5 = lambda t, b, l: t.reshape(1, b, l, FOX_HEADS, FOX_HEAD_DIM)
    return (
        y_p.reshape(bp, seq, D_MODEL),
        y_s.reshape(nb, 1, D_MODEL),
        heads5(pr["k"], bp, seq),
        heads5(pr["v"], bp, seq),
        lf_p.reshape(1, bp, seq, FOX_HEADS),
        mk_p.reshape(1, bp, n_mem, MEM_HEADS, MEM_HEAD_DIM),
        mv_p.reshape(1, bp, n_mem, MEM_HEADS, MEM_HEAD_DIM),
        st_p.reshape(1, bp, SSD_HEADS, SSD_HEAD_DIM, SSD_STATE),
        pr["xbc"].reshape(bp, seq, SSD_CONV_DIM)[:, seq - (SSD_CONV - 1):][None],
        fbuf_p[None],
        heads5(sm["k"], nb, 1),
        heads5(sm["v"], nb, 1),
        lf_s.reshape(1, nb, 1, FOX_HEADS),
        st_s.reshape(1, nb, SSD_HEADS, SSD_HEAD_DIM, SSD_STATE),
        jnp.concatenate([state_ssd_conv[0][:, 1:], sm["xbc"][:, None, :]], axis=1)[None],
        jnp.concatenate([state_ffn_conv[0][:, 1:], g_s[:, None, :]], axis=1)[None],
    )
```

```python
import functools

import jax
import jax.numpy as jnp
from jax import lax
from jax.experimental import pallas as pl
from jax.experimental.pallas import tpu as pltpu

F32 = jnp.float32
BF16 = jnp.bfloat16

D_MODEL = 2048
SSD_INNER = 4096
SSD_HEAD_DIM = 64
SSD_HEADS = 64
SSD_GROUPS = 8
SSD_STATE = 128
SSD_CONV = 4
SSD_CHUNK = 128
SSD_CONV_DIM = SSD_INNER + 2 * SSD_GROUPS * SSD_STATE
HEADS_PER_GROUP = SSD_HEADS // SSD_GROUPS
GROUP_WIDTH = HEADS_PER_GROUP * SSD_HEAD_DIM
FOX_HEAD_DIM = 128
FOX_HEADS = 16
FOX_WIDTH = FOX_HEADS * FOX_HEAD_DIM
MEM_HEADS = 4
MEM_HEAD_DIM = 128
MEM_WIDTH = MEM_HEADS * MEM_HEAD_DIM
D_FF = 5632
FFN_CONV = 3
N_BRANCH = 3
EPS = 1e-6

LANES = 128
SUBLANES = 8
VMEM_LIMIT_BYTES = 56 * 1024 * 1024

DTF_WIDTH = LANES
PAGES_PER_STEP = 8


def _cparams(*semantics):
    return pltpu.CompilerParams(dimension_semantics=semantics, vmem_limit_bytes=VMEM_LIMIT_BYTES)


def _dot(a, b):
    return jnp.dot(a, b, preferred_element_type=F32)


def _dot_nt(a, b):
    return lax.dot_general(a, b, (((1,), (1,)), ((), ())), preferred_element_type=F32)


def _silu(x):
    return x * jax.nn.sigmoid(x)


def _softplus(x):
    return jnp.maximum(x, 0.0) + jnp.log1p(jnp.exp(-jnp.abs(x)))


def _split3(x):
    hi = x.astype(BF16)
    r1 = x - hi.astype(F32)
    mid = r1.astype(BF16)
    lo = (r1 - mid.astype(F32)).astype(BF16)
    return hi, mid, lo


def _rms_kernel(x_ref, g_ref, o_ref):
    x = x_ref[...]
    ms = jnp.mean(x * x, axis=-1, keepdims=True)
    o_ref[...] = (x * lax.rsqrt(ms + EPS) * g_ref[...]).astype(o_ref.dtype)


def rms_norm(x, g, *, tm, out_dtype=BF16):
    m, d = x.shape
    return pl.pallas_call(
        _rms_kernel,
        grid=(m // tm,),
        in_specs=[pl.BlockSpec((tm, d), lambda i: (i, 0)), pl.BlockSpec((1, d), lambda i: (0, 0))],
        out_specs=pl.BlockSpec((tm, d), lambda i: (i, 0)),
        out_shape=jax.ShapeDtypeStruct((m, d), out_dtype),
        compiler_params=_cparams("parallel"),
        name="rms_norm",
    )(x, g.reshape(1, d))


def _mm_kernel(*refs, epilogue, n_rows):
    a_ref, w_ref = refs[0], refs[1]
    rows = [r[...] for r in refs[2:2 + n_rows]]
    outs = refs[2 + n_rows:]
    acc = _dot(a_ref[...], w_ref[...])
    vals = epilogue(acc, *rows)
    for o, v in zip(outs, vals):
        o[...] = v.astype(o.dtype)


def matmul(a, w, *, tm, tn, out_dtypes, epilogue=lambda acc: (acc,), rows=(), name="matmul"):
    m, k = a.shape
    n = w.shape[1]
    tm = min(tm, m)
    tn = min(tn, n)
    in_specs = [pl.BlockSpec((tm, k), lambda i, j: (i, 0)), pl.BlockSpec((k, tn), lambda i, j: (0, j))]
    in_specs += [pl.BlockSpec((1, tn), lambda i, j: (0, j)) for _ in rows]
    outs = pl.pallas_call(
        functools.partial(_mm_kernel, epilogue=epilogue, n_rows=len(rows)),
        grid=(m // tm, n // tn),
        in_specs=in_specs,
        out_specs=[pl.BlockSpec((tm, tn), lambda i, j: (i, j)) for _ in out_dtypes],
        out_shape=[jax.ShapeDtypeStruct((m, n), dt) for dt in out_dtypes],
        compiler_params=_cparams("parallel", "arbitrary"),
        name=name,
    )(a, w, *rows)
    return outs


def _dtf_epilogue(acc, bias):
    x = acc + bias
    lane = lax.broadcasted_iota(jnp.int32, x.shape, 1)
    return (jnp.where(lane < SSD_HEADS, _softplus(x), -_softplus(-x)),)


def _cumsum_kernel(x_ref, o_ref):
    x = x_ref[...]
    n = x.shape[0]
    row = lax.broadcasted_iota(jnp.int32, x.shape, 0)
    s = 1
    while s < n:
        x = x + jnp.where(row >= s, pltpu.roll(x, s, 0), 0.0)
        s *= 2
    o_ref[...] = x


def seq_cumsum(x, *, seq):
    m, d = x.shape
    return pl.pallas_call(
        _cumsum_kernel,
        grid=(m // seq,),
        in_specs=[pl.BlockSpec((seq, d), lambda b: (b, 0))],
        out_specs=pl.BlockSpec((seq, d), lambda b: (b, 0)),
        out_shape=jax.ShapeDtypeStruct((m, d), F32),
        compiler_params=_cparams("parallel"),
        name="seq_cumsum",
    )(x)


def _pair_cols(mat, pair, lane_lo):
    a = mat[:, 2 * pair:2 * pair + 1]
    b = mat[:, 2 * pair + 1:2 * pair + 2]
    return jnp.where(lane_lo, a, b)


def _ssd_kernel(xbc_ref, dtf_ref, zs_ref, cw_ref, cb_ref, a_ref, dskip_ref, gn_ref,
                o_ref, st_out_ref, xbuf, xc, st, y_scr, xw_scr):
    c = pl.program_id(1)
    nc = pl.num_programs(1)
    cs = SSD_CHUNK
    halo = SUBLANES

    @pl.when(c == 0)
    def _():
        xbuf[0:halo, :] = jnp.zeros((halo, SSD_CONV_DIM), F32)
        st[...] = jnp.zeros_like(st)

    @pl.when(c > 0)
    def _():
        xbuf[0:halo, :] = xbuf[cs:cs + halo, :]

    xbuf[halo:halo + cs, :] = xbc_ref[...]

    cw = cw_ref[...]
    cb = cb_ref[...]
    col_chunk = 512
    for j in range(SSD_CONV_DIM // col_chunk):
        sl = slice(j * col_chunk, (j + 1) * col_chunk)
        acc = xbuf[halo - 3:halo - 3 + cs, sl] * cw[0:1, sl]
        for i in range(1, SSD_CONV):
            acc = acc + xbuf[halo - 3 + i:halo - 3 + i + cs, sl] * cw[i:i + 1, sl]
        xc[:, sl] = _silu(acc + cb[:, sl])

    dt = dtf_ref[...]
    da = dt * a_ref[...]
    r_i = lax.broadcasted_iota(jnp.int32, (cs, cs), 0)
    c_i = lax.broadcasted_iota(jnp.int32, (cs, cs), 1)
    causal = c_i <= r_i
    tril = jnp.where(causal, 1.0, 0.0)
    a_cs = jnp.dot(tril, da, preferred_element_type=F32, precision=lax.Precision.HIGHEST)
    a_cs_t = a_cs.T
    dt_t = dt.T
    last = a_cs[cs - 1:cs, :]
    ecs = jnp.exp(a_cs)
    wfac = dt * jnp.exp(last - a_cs)
    cdec = jnp.exp(last)

    lane_lo = lax.broadcasted_iota(jnp.int32, (cs, LANES), 1) < SSD_HEAD_DIM
    lane_lo_row = lane_lo[0:1, :]
    pairs_per_group = HEADS_PER_GROUP // 2
    for g in range(SSD_GROUPS):
        b_off = SSD_INNER + g * SSD_STATE
        c_off = SSD_INNER + SSD_GROUPS * SSD_STATE + g * SSD_STATE
        bg = xc[:, b_off:b_off + SSD_STATE]
        cg = xc[:, c_off:c_off + SSD_STATE].astype(BF16)
        cbm = _dot_nt(cg, bg.astype(BF16))
        yoff = _dot(cg, st[g].astype(BF16))
        cd_tiles = []
        for pr in range(pairs_per_group):
            pair = g * pairs_per_group + pr
            col0 = pair * LANES
            xs_pair = xc[:, col0:col0 + LANES]
            xs_bf = xs_pair.astype(BF16)
            yd = []
            for h in (2 * pair, 2 * pair + 1):
                seg = a_cs[:, h:h + 1] - a_cs_t[h:h + 1, :]
                decay = jnp.exp(jnp.where(causal, seg, -jnp.inf))
                mh = (cbm * decay * dt_t[h:h + 1, :]).astype(BF16)
                yd.append(_dot(mh, xs_bf))
            y_pair = jnp.where(lane_lo, yd[0], yd[1])
            y_pair = y_pair + _pair_cols(ecs, pair, lane_lo) * yoff[:, pr * LANES:(pr + 1) * LANES]
            y_pair = y_pair + xs_pair * dskip_ref[:, col0:col0 + LANES]
            y_scr[:, col0:col0 + LANES] = y_pair
            xw_scr[:, pr * LANES:(pr + 1) * LANES] = (xs_pair * _pair_cols(wfac, pair, lane_lo)).astype(BF16)
            cd_tiles.append(_pair_cols(cdec, pair, lane_lo_row))
        cd_row = jnp.concatenate(cd_tiles, axis=1)
        st[g] = st[g] * cd_row + _dot(bg.T.astype(BF16), xw_scr[...])

    v = y_scr[...] * zs_ref[...]
    ms = jnp.mean(v * v, axis=-1, keepdims=True)
    o_ref[...] = (v * lax.rsqrt(ms + EPS) * gn_ref[...]).astype(o_ref.dtype)

    @pl.when(c == nc - 1)
    def _():
        for g in range(SSD_GROUPS):
            st_out_ref[0, g * GROUP_WIDTH:(g + 1) * GROUP_WIDTH, :] = st[g].T


def ssd_prompt(xbc, dtf, zs, conv_w, conv_b, a_row, dskip_row, gnorm, *, batch, seq):
    nc = seq // SSD_CHUNK
    cs = SSD_CHUNK
    row = lambda b, c: (b * nc + c, 0)
    const = lambda b, c: (0, 0)
    return pl.pallas_call(
        _ssd_kernel,
        grid=(batch, nc),
        in_specs=[
            pl.BlockSpec((cs, SSD_CONV_DIM), row),
            pl.BlockSpec((cs, DTF_WIDTH), row),
            pl.BlockSpec((cs, SSD_INNER), row),
            pl.BlockSpec((SSD_CONV, SSD_CONV_DIM), const),
            pl.BlockSpec((1, SSD_CONV_DIM), const),
            pl.BlockSpec((1, DTF_WIDTH), const),
            pl.BlockSpec((1, SSD_INNER), const),
            pl.BlockSpec((1, SSD_INNER), const),
        ],
        out_specs=[
            pl.BlockSpec((cs, SSD_INNER), row),
            pl.BlockSpec((1, SSD_HEADS * SSD_HEAD_DIM, SSD_STATE), lambda b, c: (b, 0, 0)),
        ],
        out_shape=[
            jax.ShapeDtypeStruct((batch * seq, SSD_INNER), BF16),
            jax.ShapeDtypeStruct((batch, SSD_HEADS * SSD_HEAD_DIM, SSD_STATE), F32),
        ],
        scratch_shapes=[
            pltpu.VMEM((cs + SUBLANES, SSD_CONV_DIM), F32),
            pltpu.VMEM((cs, SSD_CONV_DIM), F32),
            pltpu.VMEM((SSD_GROUPS, SSD_STATE, GROUP_WIDTH), F32),
            pltpu.VMEM((cs, SSD_INNER), F32),
            pltpu.VMEM((cs, GROUP_WIDTH), BF16),
        ],
        compiler_params=_cparams("parallel", "arbitrary"),
        name="ssd_prompt",
    )(xbc, dtf, zs, conv_w, conv_b, a_row, dskip_row, gnorm)


def _fox_kernel(q_ref, k_ref, v_ref, cq_ref, ck_ref, o_ref, *, seq, row_blocks, heads, scale):
    dh = FOX_HEAD_DIM
    blk = seq // row_blocks
    for h in range(heads):
        cols = slice(h * dh, (h + 1) * dh)
        for r in range(row_blocks):
            r0, kw = r * blk, (r + 1) * blk
            s = _dot_nt(q_ref[r0:r0 + blk, cols], k_ref[0:kw, cols]) * scale
            s = s + cq_ref[0, h, r0:r0 + blk, :] - ck_ref[0, h, :, 0:kw]
            r_i = lax.broadcasted_iota(jnp.int32, (blk, kw), 0) + r0
            c_i = lax.broadcasted_iota(jnp.int32, (blk, kw), 1)
            s = jnp.where(c_i <= r_i, s, -jnp.inf)
            p = jnp.exp(s - jnp.max(s, axis=-1, keepdims=True))
            l = jnp.sum(p, axis=-1, keepdims=True)
            o_ref[r0:r0 + blk, cols] = (_dot(p.astype(BF16), v_ref[0:kw, cols]) / l).astype(o_ref.dtype)


def fox_prompt(q, k, v, cq, ck, *, batch, seq, row_blocks, heads):
    width = heads * FOX_HEAD_DIM
    tokens = pl.BlockSpec((seq, width), lambda b, h: (b, h))
    return pl.pallas_call(
        functools.partial(_fox_kernel, seq=seq, row_blocks=row_blocks, heads=heads, scale=FOX_HEAD_DIM ** -0.5),
        grid=(batch, FOX_HEADS // heads),
        in_specs=[tokens, tokens, tokens,
                  pl.BlockSpec((1, heads, seq, 1), lambda b, h: (b, h, 0, 0)),
                  pl.BlockSpec((1, heads, 1, seq), lambda b, h: (b, h, 0, 0))],
        out_specs=tokens,
        out_shape=jax.ShapeDtypeStruct((batch * seq, FOX_WIDTH), BF16),
        compiler_params=_cparams("parallel", "arbitrary"),
        name="fox_prompt",
    )(q, k, v, cq, ck)


def _mem_kernel(q_ref, k_ref, v_ref, o_ref, *, scale):
    dh = MEM_HEAD_DIM
    for h in range(MEM_HEADS):
        cols = slice(h * dh, (h + 1) * dh)
        s = _dot_nt(q_ref[:, cols], k_ref[:, cols]) * scale
        e = jnp.exp(s - jnp.max(s, axis=-1, keepdims=True))
        p = e / jnp.sum(e, axis=-1, keepdims=True)
        o_ref[:, cols] = _dot(p.astype(BF16), v_ref[:, cols]).astype(o_ref.dtype)


def mem_prompt_attend(q, k, v, *, batch, seq, n_mem, tq):
    nq = seq // tq
    return pl.pallas_call(
        functools.partial(_mem_kernel, scale=MEM_HEAD_DIM ** -0.5),
        grid=(batch, nq),
        in_specs=[
            pl.BlockSpec((tq, MEM_WIDTH), lambda b, i: (b * nq + i, 0)),
            pl.BlockSpec((n_mem, MEM_WIDTH), lambda b, i: (b, 0)),
            pl.BlockSpec((n_mem, MEM_WIDTH), lambda b, i: (b, 0)),
        ],
        out_specs=pl.BlockSpec((tq, MEM_WIDTH), lambda b, i: (b * nq + i, 0)),
        out_shape=jax.ShapeDtypeStruct((batch * seq, MEM_WIDTH), BF16),
        compiler_params=_cparams("parallel", "arbitrary"),
        name="mem_prompt_attend",
    )(q, k, v)


def _merge_kernel(os_ref, of_ref, om_ref, ws_ref, wf_ref, wm_ref, g0_ref, g1_ref, g2_ref, o_ref):
    u = g0_ref[...] * _dot(os_ref[...], ws_ref[...])
    u = u + g1_ref[...] * _dot(of_ref[...], wf_ref[...])
    u = u + g2_ref[...] * _dot(om_ref[...], wm_ref[...])
    o_ref[...] = u.astype(o_ref.dtype)


def branch_merge(o_ssd, o_fox, o_mem, w_s, w_f, w_m, gates, *, tm, tn):
    m = o_ssd.shape[0]
    tm = min(tm, m)
    nj = D_MODEL // tn
    act = lambda k: pl.BlockSpec((tm, k), lambda i, j: (i, 0))
    wgt = lambda k: pl.BlockSpec((k, tn), lambda i, j: (0, j))
    gate = lambda br: pl.BlockSpec((tm, tn), lambda i, j: (i, br * nj + j))
    return pl.pallas_call(
        _merge_kernel,
        grid=(m // tm, nj),
        in_specs=[act(SSD_INNER), act(FOX_WIDTH), act(MEM_WIDTH), wgt(SSD_INNER), wgt(FOX_WIDTH), wgt(MEM_WIDTH),
                  gate(0), gate(1), gate(2)],
        out_specs=pl.BlockSpec((tm, tn), lambda i, j: (i, j)),
        out_shape=jax.ShapeDtypeStruct((m, D_MODEL), BF16),
        compiler_params=_cparams("parallel", "arbitrary"),
        name="branch_merge",
    )(o_ssd, o_fox, o_mem, w_s, w_f, w_m, gates, gates, gates)


def _outproj_kernel(a_ref, w_ref, x_ref, g_ref, h_ref, hn_ref):
    h = x_ref[...] + _dot(a_ref[...], w_ref[...])
    h_ref[...] = h
    ms = jnp.mean(h * h, axis=-1, keepdims=True)
    hn_ref[...] = (h * lax.rsqrt(ms + EPS) * g_ref[...]).astype(hn_ref.dtype)


def out_proj(merged, w_out, x, g_ffn, *, tm):
    m = x.shape[0]
    tm = min(tm, m)
    row = lambda i: (i, 0)
    const = lambda i: (0, 0)
    return pl.pallas_call(
        _outproj_kernel,
        grid=(m // tm,),
        in_specs=[pl.BlockSpec((tm, D_MODEL), row), pl.BlockSpec((D_MODEL, D_MODEL), const),
                  pl.BlockSpec((tm, D_MODEL), row), pl.BlockSpec((1, D_MODEL), const)],
        out_specs=[pl.BlockSpec((tm, D_MODEL), row), pl.BlockSpec((tm, D_MODEL), row)],
        out_shape=[jax.ShapeDtypeStruct((m, D_MODEL), F32), jax.ShapeDtypeStruct((m, D_MODEL), BF16)],
        compiler_params=_cparams("parallel"),
        name="out_proj",
    )(merged, w_out, x, g_ffn.reshape(1, D_MODEL))


def _ffn_kernel(hn_ref, wg_ref, wu_ref, cw_ref, cb_ref, act_ref, buf_ref, gbuf, *, tiles_per_seq):
    i = pl.program_id(1)
    tm = hn_ref.shape[0]
    halo = SUBLANES

    @pl.when(i % tiles_per_seq == 0)
    def _():
        gbuf[0:halo, :] = jnp.zeros((halo, gbuf.shape[1]), F32)

    @pl.when(i % tiles_per_seq != 0)
    def _():
        gbuf[0:halo, :] = gbuf[tm:tm + halo, :]

    hn = hn_ref[...]
    gbuf[halo:halo + tm, :] = _dot(hn, wg_ref[...])
    cw = cw_ref[...]
    conv = gbuf[halo - 2:halo - 2 + tm, :] * cw[0:1, :]
    conv = conv + gbuf[halo - 1:halo - 1 + tm, :] * cw[1:2, :]
    conv = conv + gbuf[halo:halo + tm, :] * cw[2:3, :]
    conv = conv + cb_ref[...]
    act_ref[...] = (_silu(conv) * _dot(hn, wu_ref[...])).astype(act_ref.dtype)
    buf_ref[0] = gbuf[halo + tm - 2:halo + tm, :]


def ffn_gate_up(hn, w_gate, w_up, conv_w, conv_b, *, batch, seq, tm, tn):
    m = hn.shape[0]
    tiles_per_seq = seq // tm
    return pl.pallas_call(
        functools.partial(_ffn_kernel, tiles_per_seq=tiles_per_seq),
        grid=(D_FF // tn, m // tm),
        in_specs=[
            pl.BlockSpec((tm, D_MODEL), lambda j, i: (i, 0)),
            pl.BlockSpec((D_MODEL, tn), lambda j, i: (0, j)),
            pl.BlockSpec((D_MODEL, tn), lambda j, i: (0, j)),
            pl.BlockSpec((FFN_CONV, tn), lambda j, i: (0, j)),
            pl.BlockSpec((1, tn), lambda j, i: (0, j)),
        ],
        out_specs=[
            pl.BlockSpec((tm, tn), lambda j, i: (i, j)),
            pl.BlockSpec((1, FFN_CONV - 1, tn), lambda j, i: (i // tiles_per_seq, 0, j)),
        ],
        out_shape=[
            jax.ShapeDtypeStruct((m, D_FF), BF16),
            jax.ShapeDtypeStruct((batch, FFN_CONV - 1, D_FF), F32),
        ],
        scratch_shapes=[pltpu.VMEM((tm + SUBLANES, tn), F32)],
        compiler_params=_cparams("parallel", "arbitrary"),
        name="ffn_gate_up",
    )(hn, w_gate, w_up, conv_w, conv_b)


def _down_kernel(a_ref, w_ref, h_ref, g_ref, y_ref, acc):
    k = pl.program_id(1)

    @pl.when(k == 0)
    def _():
        acc[...] = h_ref[...]

    acc[...] += _dot(a_ref[...], w_ref[...])

    @pl.when(k == pl.num_programs(1) - 1)
    def _():
        h = acc[...]
        ms = jnp.mean(h * h, axis=-1, keepdims=True)
        y_ref[...] = h * lax.rsqrt(ms + EPS) * g_ref[...]


def ffn_down(act, w_down, h, g_final, *, tm, tk):
    m = h.shape[0]
    tm = min(tm, m)
    return pl.pallas_call(
        _down_kernel,
        grid=(m // tm, D_FF // tk),
        in_specs=[pl.BlockSpec((tm, tk), lambda i, k: (i, k)), pl.BlockSpec((tk, D_MODEL), lambda i, k: (k, 0)),
                  pl.BlockSpec((tm, D_MODEL), lambda i, k: (i, 0)), pl.BlockSpec((1, D_MODEL), lambda i, k: (0, 0))],
        out_specs=pl.BlockSpec((tm, D_MODEL), lambda i, k: (i, 0)),
        out_shape=jax.ShapeDtypeStruct((m, D_MODEL), F32),
        scratch_shapes=[pltpu.VMEM((tm, D_MODEL), F32)],
        compiler_params=_cparams("parallel", "arbitrary"),
        name="ffn_down",
    )(act, w_down, h, g_final.reshape(1, D_MODEL))


def _ssd_s_prep_kernel(xbc_ref, buf_ref, cw_ref, cb_ref, dtf_ref, a_ref, expand_ref, xc_ref, cols_ref):
    cw = cw_ref[...]
    acc = buf_ref[0] * cw[0:1, :]
    acc = acc + buf_ref[1] * cw[1:2, :]
    acc = acc + buf_ref[2] * cw[2:3, :]
    acc = acc + xbc_ref[...] * cw[3:4, :]
    xcv = _silu(acc + cb_ref[...])
    xc_ref[...] = xcv
    dt = dtf_ref[...]
    decay = jnp.exp(dt * a_ref[...])
    nb = dt.shape[0]
    parts = _split3(dt) + _split3(decay)
    stacked = jnp.concatenate(parts, axis=0)
    ex = _dot(stacked, expand_ref[...])
    dt_exp = ex[0:nb] + ex[nb:2 * nb] + ex[2 * nb:3 * nb]
    dec_exp = ex[3 * nb:4 * nb] + ex[4 * nb:5 * nb] + ex[5 * nb:6 * nb]
    xdt = xcv[:, 0:SSD_INNER] * dt_exp
    pad = jnp.zeros((LANES - 2 * nb, SSD_INNER), F32)
    cols_ref[...] = jnp.concatenate([xdt, dec_exp, pad], axis=0).T


def ssd_sample_prep(xbc, buf_t, conv_w, conv_b, dtf, a_row, expand):
    nb = xbc.shape[0]
    full = lambda s: pl.BlockSpec(s, lambda i: (0,) * len(s))
    return pl.pallas_call(
        _ssd_s_prep_kernel,
        grid=(1,),
        in_specs=[full(xbc.shape), full(buf_t.shape), full(conv_w.shape), full(conv_b.shape), full(dtf.shape),
                  full(a_row.shape), full(expand.shape)],
        out_specs=[full((nb, SSD_CONV_DIM)), full((SSD_INNER, LANES))],
        out_shape=[jax.ShapeDtypeStruct((nb, SSD_CONV_DIM), F32), jax.ShapeDtypeStruct((SSD_INNER, LANES), F32)],
        compiler_params=_cparams("arbitrary"),
        name="ssd_sample_prep",
    )(xbc, buf_t, conv_w, conv_b, dtf, a_row, expand)


def _ssd_s_step_kernel(cols_ref, h0_ref, xc_ref, zs_ref, dskip_ref, gn_ref, hn_ref, o_ref, *, nb):
    b = pl.program_id(0)
    cols = cols_ref[...]
    hi, mid, lo = _split3(cols)
    lhs = jnp.concatenate([hi, mid, lo], axis=1)
    r_i = lax.broadcasted_iota(jnp.int32, (3 * LANES, 2 * LANES), 0) % LANES
    c_i = lax.broadcasted_iota(jnp.int32, (3 * LANES, 2 * LANES), 1)
    sel = jnp.where(r_i == jnp.where(c_i < LANES, b, nb + b), 1.0, 0.0).astype(BF16)
    picked = _dot(lhs, sel)
    xb = picked[:, 0:LANES]
    db = picked[:, LANES:2 * LANES]
    xc = xc_ref[0]
    y_parts = []
    for g in range(SSD_GROUPS):
        rows = slice(g * GROUP_WIDTH, (g + 1) * GROUP_WIDTH)
        b_row = xc[:, SSD_INNER + g * SSD_STATE:SSD_INNER + (g + 1) * SSD_STATE]
        c_off = SSD_INNER + SSD_GROUPS * SSD_STATE + g * SSD_STATE
        c_row = xc[:, c_off:c_off + SSD_STATE]
        hn = h0_ref[0, rows, :] * db[rows, :] + xb[rows, :] * b_row
        hn_ref[0, rows, :] = hn
        c8 = jnp.broadcast_to(c_row, (SUBLANES, SSD_STATE)).astype(BF16)
        y_parts.append(_dot_nt(c8, hn.astype(BF16))[0:1, :])
    xs = xc[:, 0:SSD_INNER]
    y = jnp.concatenate(y_parts, axis=1) + xs * dskip_ref[...]
    v = y * zs_ref[0]
    ms = jnp.mean(v * v, axis=-1, keepdims=True)
    o_ref[0] = (v * lax.rsqrt(ms + EPS) * gn_ref[...]).astype(o_ref.dtype)


def ssd_sample_step(cols, h0, xc, zs, dskip_row, gnorm):
    nb = h0.shape[0]
    rows = SSD_HEADS * SSD_HEAD_DIM
    const2 = lambda b: (0, 0)
    per_b = lambda b: (b, 0, 0)
    return pl.pallas_call(
        functools.partial(_ssd_s_step_kernel, nb=nb),
        grid=(nb,),
        in_specs=[
            pl.BlockSpec((SSD_INNER, LANES), const2),
            pl.BlockSpec((1, rows, SSD_STATE), per_b),
            pl.BlockSpec((1, 1, SSD_CONV_DIM), per_b),
            pl.BlockSpec((1, 1, SSD_INNER), per_b),
            pl.BlockSpec((1, SSD_INNER), const2),
            pl.BlockSpec((1, SSD_INNER), const2),
        ],
        out_specs=[pl.BlockSpec((1, rows, SSD_STATE), per_b), pl.BlockSpec((1, 1, SSD_INNER), per_b)],
        out_shape=[jax.ShapeDtypeStruct((nb, rows, SSD_STATE), F32), jax.ShapeDtypeStruct((nb, 1, SSD_INNER), BF16)],
        compiler_params=_cparams("parallel"),
        name="ssd_sample_step",
    )(cols, h0, xc.reshape(nb, 1, SSD_CONV_DIM), zs.reshape(nb, 1, SSD_INNER), dskip_row, gnorm)


def _head_matched(shape, n_heads):
    r_i = lax.broadcasted_iota(jnp.int32, shape, 0)
    c_i = lax.broadcasted_iota(jnp.int32, shape, 1)
    return c_i % n_heads == r_i % n_heads


def _dense_from_natural(nat, tile):
    page, nh = nat.shape
    tiled = sum(_dot(part, tile) for part in _split3(nat))
    r_i = lax.broadcasted_iota(jnp.int32, tiled.shape, 0)
    c_i = lax.broadcasted_iota(jnp.int32, tiled.shape, 1)
    kept = jnp.where(c_i // nh == r_i % SUBLANES, tiled, 0.0)
    return jnp.sum(kept.reshape(page // SUBLANES, SUBLANES, LANES), axis=1)


def _fox_s_kernel(pt_ref, *refs, n_steps, scale):
    del pt_ref
    pp = PAGES_PER_STEP
    nh = FOX_HEADS
    q_ref, kn_ref, vn_ref, lfn_ref, suf_ref, tile_ref = refs[0:6]
    k_refs = refs[6:6 + pp]
    v_refs = refs[6 + pp:6 + 2 * pp]
    lf_refs = refs[6 + 2 * pp:6 + 3 * pp]
    o_ref = refs[6 + 3 * pp]
    m_scr, l_scr, acc_scr, carry = refs[7 + 3 * pp:]
    j = pl.program_id(1)

    @pl.when(j == 0)
    def _():
        m_scr[...] = jnp.full_like(m_scr, -jnp.inf)
        l_scr[...] = jnp.zeros_like(l_scr)
        acc_scr[...] = jnp.zeros_like(acc_scr)
        carry[...] = lfn_ref[0]

    q = q_ref[0].astype(BF16)
    page = k_refs[0].shape[2]
    rows = page * nh // LANES
    keep = _head_matched((nh, page * nh), nh)
    row_i = lax.broadcasted_iota(jnp.int32, (rows, LANES), 0)

    run = carry[...]
    scores = []
    lf_all = jnp.concatenate([lf_refs[i][0, 0] for i in range(pp)], axis=0)
    lfd_all = _dense_from_natural(lf_all, tile_ref[...])
    r_all = sum(_dot(part, suf_ref[...]) for part in _split3(lfd_all))
    for i in range(pp):
        r = r_all[i * rows:(i + 1) * rows]
        within = r[:, 0:LANES]
        rowtot = r[:, LANES:]
        incl = rowtot
        sft = 1
        while sft < rows:
            incl = incl + jnp.where(row_i + sft < rows, pltpu.roll(incl, rows - sft, 0), 0.0)
            sft *= 2
        bias_d = within + (incl - rowtot) + run
        run = run + incl[0:1, :]
        bias = jnp.concatenate([jnp.broadcast_to(bias_d[t:t + 1, :], (nh, LANES)) for t in range(rows)], axis=1)
        k2 = k_refs[i][0, 0].reshape(page * nh, FOX_HEAD_DIM).astype(BF16)
        scores.append(jnp.where(keep, _dot_nt(q, k2) * scale + bias, -jnp.inf))
    carry[...] = run

    m_old = m_scr[...]
    m_new = m_old
    for s in scores:
        m_new = jnp.maximum(m_new, jnp.max(s, axis=-1, keepdims=True))
    alpha = jnp.exp(m_old - m_new)
    l = alpha * l_scr[...]
    acc = alpha * acc_scr[...]
    for i, s in enumerate(scores):
        p = jnp.exp(s - m_new)
        l = l + jnp.sum(p, axis=-1, keepdims=True)
        v2 = v_refs[i][0, 0].reshape(page * nh, FOX_HEAD_DIM).astype(BF16)
        acc = acc + _dot(p.astype(BF16), v2)
    m_scr[...] = m_new
    l_scr[...] = l
    acc_scr[...] = acc

    @pl.when(j == n_steps - 1)
    def _():
        s_self = jnp.sum(q_ref[0] * kn_ref[0], axis=-1, keepdims=True) * scale
        m_old = m_scr[...]
        m_new = jnp.maximum(m_old, s_self)
        alpha = jnp.exp(m_old - m_new)
        p_self = jnp.exp(s_self - m_new)
        l = alpha * l_scr[...] + p_self
        acc = alpha * acc_scr[...] + p_self * vn_ref[0]
        o_ref[0] = (acc / l).astype(o_ref.dtype)


def fox_sample(page_table, q, k_new, v_new, lf_new_tiled, k_pool, v_pool, lf_pool):
    nb, n_pages = page_table.shape
    page = k_pool.shape[2]
    pp = PAGES_PER_STEP
    n_steps = n_pages // pp
    r_i = lax.broadcasted_iota(jnp.int32, (LANES, 2 * LANES), 0)
    c_i = lax.broadcasted_iota(jnp.int32, (LANES, 2 * LANES), 1)
    same = r_i % FOX_HEADS == c_i % FOX_HEADS
    suf = (same & ((c_i >= LANES) | (r_i > c_i))).astype(BF16)
    tile = same[:FOX_HEADS, :LANES].astype(BF16)

    def newest_first(i):
        return lambda b, j, pt: pt[b, n_pages - 1 - (j * pp + i)]

    def kv_spec(i):
        pick = newest_first(i)
        return pl.BlockSpec((1, 1, page, FOX_HEADS, FOX_HEAD_DIM), lambda b, j, pt: (0, pick(b, j, pt), 0, 0, 0))

    def lf_spec(i):
        pick = newest_first(i)
        return pl.BlockSpec((1, 1, page, FOX_HEADS), lambda b, j, pt: (0, pick(b, j, pt), 0, 0))

    per_b = lambda shape: pl.BlockSpec((1,) + shape, lambda b, j, pt: (b, 0, 0))
    hd = (FOX_HEADS, FOX_HEAD_DIM)
    in_specs = [per_b(hd), per_b(hd), per_b(hd), per_b((1, LANES)),
                pl.BlockSpec((LANES, 2 * LANES), lambda b, j, pt: (0, 0)),
                pl.BlockSpec((FOX_HEADS, LANES), lambda b, j, pt: (0, 0))]
    in_specs += [kv_spec(i) for i in range(pp)]
    in_specs += [kv_spec(i) for i in range(pp)]
    in_specs += [lf_spec(i) for i in range(pp)]
    grid_spec = pltpu.PrefetchScalarGridSpec(
        num_scalar_prefetch=1,
        grid=(nb, n_steps),
        in_specs=in_specs,
        out_specs=per_b(hd),
        scratch_shapes=[
            pltpu.VMEM((FOX_HEADS, 1), F32),
            pltpu.VMEM((FOX_HEADS, 1), F32),
            pltpu.VMEM(hd, F32),
            pltpu.VMEM((1, LANES), F32),
        ],
    )
    return pl.pallas_call(
        functools.partial(_fox_s_kernel, n_steps=n_steps, scale=FOX_HEAD_DIM ** -0.5),
        grid_spec=grid_spec,
        out_shape=jax.ShapeDtypeStruct((nb,) + hd, BF16),
        compiler_params=_cparams("parallel", "arbitrary"),
        name="fox_sample",
    )(page_table, q.reshape((nb,) + hd), k_new.reshape((nb,) + hd), v_new.reshape((nb,) + hd),
      lf_new_tiled.reshape(nb, 1, LANES), suf, tile,
      *([k_pool] * pp), *([v_pool] * pp), *([lf_pool] * pp))


def _mem_s_kernel(q_ref, k_ref, v_ref, o_ref, *, scale):
    n_mem = k_ref.shape[2]
    k2 = k_ref[0, 0].reshape(n_mem * MEM_HEADS, MEM_HEAD_DIM).astype(BF16)
    v2 = v_ref[0, 0].reshape(n_mem * MEM_HEADS, MEM_HEAD_DIM).astype(BF16)
    q = q_ref[0]
    keep = _head_matched((q.shape[0], n_mem * MEM_HEADS), MEM_HEADS)
    s = jnp.where(keep, _dot_nt(q, k2) * scale, -jnp.inf)
    e = jnp.exp(s - jnp.max(s, axis=-1, keepdims=True))
    p = e / jnp.sum(e, axis=-1, keepdims=True)
    o_ref[0] = _dot(p.astype(BF16), v2).astype(o_ref.dtype)


def mem_sample_attend(q_rep, k, v):
    _, nb, n_mem, _, _ = k.shape
    rep = q_rep.shape[1]
    kv = pl.BlockSpec((1, 1, n_mem, MEM_HEADS, MEM_HEAD_DIM), lambda b: (0, b, 0, 0, 0))
    return pl.pallas_call(
        functools.partial(_mem_s_kernel, scale=MEM_HEAD_DIM ** -0.5),
        grid=(nb,),
        in_specs=[pl.BlockSpec((1, rep, MEM_HEAD_DIM), lambda b: (b, 0, 0)), kv, kv],
        out_specs=pl.BlockSpec((1, rep, MEM_HEAD_DIM), lambda b: (b, 0, 0)),
        out_shape=jax.ShapeDtypeStruct((nb, rep, MEM_HEAD_DIM), BF16),
        compiler_params=_cparams("parallel"),
        name="mem_sample_attend",
    )(q_rep, k, v)


def _ffn_s_kernel(g_ref, u_ref, buf_ref, cw_ref, cb_ref, o_ref):
    cw = cw_ref[...]
    conv = buf_ref[0] * cw[0:1, :]
    conv = conv + buf_ref[1] * cw[1:2, :]
    conv = conv + g_ref[...] * cw[2:3, :]
    conv = conv + cb_ref[...]
    o_ref[...] = (_silu(conv) * u_ref[...]).astype(o_ref.dtype)


def ffn_sample_act(g, u, buf_t, conv_w, conv_b):
    full = lambda s: pl.BlockSpec(s, lambda i: (0,) * len(s))
    return pl.pallas_call(
        _ffn_s_kernel,
        grid=(1,),
        in_specs=[full(g.shape), full(u.shape), full(buf_t.shape), full(conv_w.shape), full(conv_b.shape)],
        out_specs=full(g.shape),
        out_shape=jax.ShapeDtypeStruct(g.shape, BF16),
        compiler_params=_cparams("arbitrary"),
        name="ffn_sample_act",
    )(g, u, buf_t, conv_w, conv_b)


def _project(xn, w, *, tm, tn):
    mm = functools.partial(matmul, xn, tm=tm)
    (zs,) = mm(w["z"], tn=tn, out_dtypes=(F32,), epilogue=lambda acc: (_silu(acc),), name="proj_z")
    (xbc,) = mm(w["xbc"], tn=tn, out_dtypes=(F32,), name="proj_xbc")
    (dtf,) = mm(w["dtf"], tn=DTF_WIDTH, out_dtypes=(F32,), epilogue=_dtf_epilogue, rows=(w["dtf_bias"],),
                name="proj_dtf")
    (q,) = mm(w["q"], tn=tn, out_dtypes=(BF16,), name="proj_q")
    k, k_bf = mm(w["k"], tn=tn, out_dtypes=(F32, BF16), epilogue=lambda acc: (acc, acc), name="proj_k")
    v, v_bf = mm(w["v"], tn=tn, out_dtypes=(F32, BF16), epilogue=lambda acc: (acc, acc), name="proj_v")
    (mq,) = mm(w["mq"], tn=tn, out_dtypes=(BF16,), name="proj_mq")
    (gates,) = mm(w["gates"], tn=tn, out_dtypes=(F32,), epilogue=lambda acc: (jax.nn.sigmoid(acc),),
                  name="proj_gates")
    return dict(zs=zs, xbc=xbc, dtf=dtf, q=q, k=k, k_bf=k_bf, v=v, v_bf=v_bf, mq=mq, gates=gates)


def kernel(x_prompt, x_sample, cache_fox_k, cache_fox_v, cache_fox_logf, cache_mem_k, cache_mem_v, state_ssd,
           state_ssd_conv, state_ffn_conv, page_table, mem_prompt, g_mix, w_in, ssd_conv_w, ssd_conv_b,
           ssd_dt_bias, ssd_a_log, ssd_d, ssd_g_norm, fox_b_forget, g_mem, w_mem_kv, w_branch, w_out, g_ffn,
           w_ffn_gate, w_ffn_up, ffn_conv_w, ffn_conv_b, w_ffn_down, g_final):
    depth = w_in.shape[0]
    assert depth == 1, "single-layer trunk"
    bp, seq, d = x_prompt.shape
    nb, dec_seq, _ = x_sample.shape
    assert d == D_MODEL and dec_seq == 1 and seq % SSD_CHUNK == 0
    n_mem = mem_prompt.shape[1]
    n_pool, page = cache_fox_k.shape[1], cache_fox_k.shape[2]
    assert page_table.shape[1] % PAGES_PER_STEP == 0
    mp = bp * seq

    wi = w_in[0]
    offs = [0]
    for width in (SSD_INNER, SSD_CONV_DIM, SSD_HEADS, FOX_WIDTH, FOX_WIDTH, FOX_WIDTH, FOX_HEADS, MEM_WIDTH,
                  N_BRANCH * D_MODEL):
        offs.append(offs[-1] + width)
    assert offs[-1] == wi.shape[1]
    seg = lambda i: wi[:, offs[i]:offs[i + 1]].astype(BF16)
    dtf_pad = DTF_WIDTH - SSD_HEADS - FOX_HEADS
    w = dict(
        z=seg(0), xbc=seg(1), q=seg(3), k=seg(4), v=seg(5), mq=seg(7), gates=seg(8),
        dtf=jnp.concatenate([seg(2), seg(6), jnp.zeros((D_MODEL, dtf_pad), BF16)], axis=1),
        dtf_bias=jnp.concatenate([ssd_dt_bias[0], fox_b_forget[0], jnp.zeros((dtf_pad,), F32)]).reshape(1, DTF_WIDTH),
    )
    wb = w_branch[0]
    w_s = wb[:SSD_INNER].astype(BF16)
    w_f = wb[SSD_INNER:SSD_INNER + FOX_WIDTH].astype(BF16)
    w_m = wb[SSD_INNER + FOX_WIDTH:].astype(BF16)
    w_o = w_out[0].astype(BF16)
    w_g = w_ffn_gate[0].astype(BF16)
    w_u = w_ffn_up[0].astype(BF16)
    w_d = w_ffn_down[0].astype(BF16)
    w_mk = w_mem_kv[0][:, :MEM_WIDTH].astype(BF16)
    w_mv = w_mem_kv[0][:, MEM_WIDTH:].astype(BF16)
    a_row = jnp.concatenate([-jnp.exp(ssd_a_log[0]), jnp.zeros((DTF_WIDTH - SSD_HEADS,), F32)]).reshape(1, DTF_WIDTH)
    dskip_row = jnp.repeat(ssd_d[0], SSD_HEAD_DIM).reshape(1, SSD_INNER)
    gnorm_row = ssd_g_norm[0].reshape(1, SSD_INNER)
    conv_w = ssd_conv_w[0]
    conv_b = ssd_conv_b[0].reshape(1, SSD_CONV_DIM)
    fconv_w = ffn_conv_w[0]
    fconv_b = ffn_conv_b[0].reshape(1, D_FF)

    xp = x_prompt.reshape(mp, D_MODEL)
    pr = _project(rms_norm(xp, g_mix[0], tm=512), w, tm=2048, tn=512)

    mem_n = rms_norm(mem_prompt.reshape(bp * n_mem, D_MODEL), g_mem[0], tm=512)
    mk_p, mk_bf = matmul(mem_n, w_mk, tm=1024, tn=512, out_dtypes=(F32, BF16), epilogue=lambda acc: (acc, acc),
                         name="proj_mem_k")
    mv_p, mv_bf = matmul(mem_n, w_mv, tm=1024, tn=512, out_dtypes=(F32, BF16), epilogue=lambda acc: (acc, acc),
                         name="proj_mem_v")

    o_ssd_p, st_p = ssd_prompt(pr["xbc"], pr["dtf"], pr["zs"], conv_w, conv_b, a_row, dskip_row, gnorm_row,
                               batch=bp, seq=seq)

    c_all = seq_cumsum(pr["dtf"], seq=seq)
    c_p = c_all[:, SSD_HEADS:SSD_HEADS + FOX_HEADS].reshape(bp, seq, FOX_HEADS).transpose(0, 2, 1)
    o_fox_p = fox_prompt(pr["q"], pr["k_bf"], pr["v_bf"], c_p.reshape(bp, FOX_HEADS, seq, 1),
                         c_p.reshape(bp, FOX_HEADS, 1, seq), batch=bp, seq=seq, row_blocks=4, heads=4)
    o_mem_p = mem_prompt_attend(pr["mq"], mk_bf, mv_bf, batch=bp, seq=seq, n_mem=n_mem, tq=1024)

    merged_p = branch_merge(o_ssd_p, o_fox_p, o_mem_p, w_s, w_f, w_m, pr["gates"], tm=1024, tn=256)
    h_p, hn_p = out_proj(merged_p, w_o, xp, g_ffn[0], tm=512)
    act_p, fbuf_p = ffn_gate_up(hn_p, w_g, w_u, fconv_w, fconv_b, batch=bp, seq=seq, tm=1024, tn=512)
    y_p = ffn_down(act_p, w_d, h_p, g_final, tm=512, tk=D_FF // 2)

    xs_ = x_sample.reshape(nb, D_MODEL)
    sm = _project(rms_norm(xs_, g_mix[0], tm=nb), w, tm=nb, tn=2048)

    buf_t = state_ssd_conv[0].transpose(1, 0, 2)
    h_i = lax.broadcasted_iota(jnp.int32, (DTF_WIDTH, SSD_INNER), 0)
    col_i = lax.broadcasted_iota(jnp.int32, (DTF_WIDTH, SSD_INNER), 1)
    expand = (col_i // SSD_HEAD_DIM == h_i).astype(BF16)
    xc_s, cols_s = ssd_sample_prep(sm["xbc"], buf_t, conv_w, conv_b, sm["dtf"], a_row, expand)
    st_s, o_ssd_s = ssd_sample_step(cols_s, state_ssd[0].reshape(nb, SSD_HEADS * SSD_HEAD_DIM, SSD_STATE), xc_s,
                                    sm["zs"], dskip_row, gnorm_row)

    lf_s = sm["dtf"][:, SSD_HEADS:SSD_HEADS + FOX_HEADS]
    o_fox_s = fox_sample(page_table, sm["q"].astype(F32), sm["k"], sm["v"], jnp.tile(lf_s, (1, LANES // FOX_HEADS)),
                         cache_fox_k, cache_fox_v, cache_fox_logf)
    mq_rep = jnp.tile(sm["mq"].reshape(nb, MEM_HEADS, MEM_HEAD_DIM), (1, 2 * SUBLANES // MEM_HEADS, 1))
    o_mem_s = mem_sample_attend(mq_rep, cache_mem_k, cache_mem_v)
    o_mem_s = o_mem_s[:, :MEM_HEADS].reshape(nb, MEM_WIDTH)

    merged_s = branch_merge(o_ssd_s.reshape(nb, SSD_INNER), o_fox_s.reshape(nb, FOX_WIDTH), o_mem_s, w_s, w_f, w_m,
                            sm["gates"], tm=nb, tn=1024)
    h_s, hn_s = out_proj(merged_s, w_o, xs_, g_ffn[0], tm=nb)
    (g_s,) = matmul(hn_s, w_g, tm=nb, tn=D_FF // 2, out_dtypes=(F32,), name="ffn_gate_s")
    (u_s,) = matmul(hn_s, w_u, tm=nb, tn=D_FF // 2, out_dtypes=(F32,), name="ffn_up_s")
    act_s = ffn_sample_act(g_s, u_s, state_ffn_conv[0].transpose(1, 0, 2), fconv_w, fconv_b)
    y_s = ffn_down(act_s, w_d, h_s, g_final, tm=nb, tk=D_FF // 2)

    lf_p = pr["dtf"][:, SSD_HEADS:SSD_HEADS + FOX_HEADS]
    heads5 = lambda t, b, l: t.reshape(1, b, l, FOX_HEADS, FOX_HEAD_DIM)
    return (
        y_p.reshape(bp, seq, D_MODEL),
        y_s.reshape(nb, 1, D_MODEL),
        heads5(pr["k"], bp, seq),
        heads5(pr["v"], bp, seq),
        lf_p.reshape(1, bp, seq, FOX_HEADS),
        mk_p.reshape(1, bp, n_mem, MEM_HEADS, MEM_HEAD_DIM),
        mv_p.reshape(1, bp, n_mem, MEM_HEADS, MEM_HEAD_DIM),
        st_p.reshape(1, bp, SSD_HEADS, SSD_HEAD_DIM, SSD_STATE),
        pr["xbc"].reshape(bp, seq, SSD_CONV_DIM)[:, seq - (SSD_CONV - 1):][None],
        fbuf_p[None],
        heads5(sm["k"], nb, 1),
        heads5(sm["v"], nb, 1),
        lf_s.reshape(1, nb, 1, FOX_HEADS),
        st_s.reshape(1, nb, SSD_HEADS, SSD_HEAD_DIM, SSD_STATE),
        jnp.concatenate([state_ssd_conv[0][:, 1:], sm["xbc"][:, None, :]], axis=1)[None],
        jnp.concatenate([state_ffn_conv[0][:, 1:], g_s[:, None, :]], axis=1)[None],
    )
```

```python
import functools

import jax
import jax.numpy as jnp
from jax import lax
from jax.experimental import pallas as pl
from jax.experimental.pallas import tpu as pltpu

F32 = jnp.float32
BF16 = jnp.bfloat16

D_MODEL = 2048
SSD_INNER = 4096
SSD_HEAD_DIM = 64
SSD_HEADS = 64
SSD_GROUPS = 8
SSD_STATE = 128
SSD_CONV = 4
SSD_CHUNK = 128
SSD_CONV_DIM = SSD_INNER + 2 * SSD_GROUPS * SSD_STATE
HEADS_PER_GROUP = SSD_HEADS // SSD_GROUPS
GROUP_WIDTH = HEADS_PER_GROUP * SSD_HEAD_DIM
FOX_HEAD_DIM = 128
FOX_HEADS = 16
FOX_WIDTH = FOX_HEADS * FOX_HEAD_DIM
MEM_HEADS = 4
MEM_HEAD_DIM = 128
MEM_WIDTH = MEM_HEADS * MEM_HEAD_DIM
D_FF = 5632
FFN_CONV = 3
N_BRANCH = 3
EPS = 1e-6

LANES = 128
SUBLANES = 8
VMEM_LIMIT_BYTES = 56 * 1024 * 1024

DTF_WIDTH = LANES
PAGES_PER_STEP = 8


def _cparams(*semantics):
    return pltpu.CompilerParams(dimension_semantics=semantics, vmem_limit_bytes=VMEM_LIMIT_BYTES)


def _dot(a, b):
    return jnp.dot(a, b, preferred_element_type=F32)


def _dot_nt(a, b):
    return lax.dot_general(a, b, (((1,), (1,)), ((), ())), preferred_element_type=F32)


def _silu(x):
    return x * jax.nn.sigmoid(x)


def _softplus(x):
    return jnp.maximum(x, 0.0) + jnp.log1p(jnp.exp(-jnp.abs(x)))


def _split3(x):
    hi = x.astype(BF16)
    r1 = x - hi.astype(F32)
    mid = r1.astype(BF16)
    lo = (r1 - mid.astype(F32)).astype(BF16)
    return hi, mid, lo


def _rms_kernel(x_ref, g_ref, o_ref):
    x = x_ref[...]
    ms = jnp.mean(x * x, axis=-1, keepdims=True)
    o_ref[...] = (x * lax.rsqrt(ms + EPS) * g_ref[...]).astype(o_ref.dtype)


def rms_norm(x, g, *, tm, out_dtype=BF16):
    m, d = x.shape
    return pl.pallas_call(
        _rms_kernel,
        grid=(m // tm,),
        in_specs=[pl.BlockSpec((tm, d), lambda i: (i, 0)), pl.BlockSpec((1, d), lambda i: (0, 0))],
        out_specs=pl.BlockSpec((tm, d), lambda i: (i, 0)),
        out_shape=jax.ShapeDtypeStruct((m, d), out_dtype),
        compiler_params=_cparams("parallel"),
        name="rms_norm",
    )(x, g.reshape(1, d))


def _mm_kernel(*refs, epilogue, n_rows):
    a_ref, w_ref = refs[0], refs[1]
    rows = [r[...] for r in refs[2:2 + n_rows]]
    outs = refs[2 + n_rows:]
    acc = _dot(a_ref[...], w_ref[...].astype(BF16))
    vals = epilogue(acc, *rows)
    for o, v in zip(outs, vals):
        o[...] = v.astype(o.dtype)


def matmul(a, w, *, tm, tn, out_dtypes, epilogue=lambda acc: (acc,), rows=(), name="matmul", cols=None):
    m, k = a.shape
    tm = min(tm, m)
    if cols is None:
        n = w.shape[1]
        tn = min(tn, n)
        w_spec = pl.BlockSpec((k, tn), lambda i, j: (0, j))
    else:
        start, n = cols
        tn = min(tn, n)
        assert start % tn == 0 and n % tn == 0
        first = start // tn
        w_spec = pl.BlockSpec((None, k, tn), lambda i, j: (0, 0, first + j))
    in_specs = [pl.BlockSpec((tm, k), lambda i, j: (i, 0)), w_spec]
    in_specs += [pl.BlockSpec((1, tn), lambda i, j: (0, j)) for _ in rows]
    outs = pl.pallas_call(
        functools.partial(_mm_kernel, epilogue=epilogue, n_rows=len(rows)),
        grid=(m // tm, n // tn),
        in_specs=in_specs,
        out_specs=[pl.BlockSpec((tm, tn), lambda i, j: (i, j)) for _ in out_dtypes],
        out_shape=[jax.ShapeDtypeStruct((m, n), dt) for dt in out_dtypes],
        compiler_params=_cparams("parallel", "arbitrary"),
        name=name,
    )(a, w, *rows)
    return outs


def _dtf_epilogue(acc, bias):
    x = acc + bias
    lane = lax.broadcasted_iota(jnp.int32, x.shape, 1)
    return (jnp.where(lane < SSD_HEADS, _softplus(x), -_softplus(-x)),)


def _cumsum_kernel(x_ref, o_ref):
    x = x_ref[...]
    n = x.shape[0]
    row = lax.broadcasted_iota(jnp.int32, x.shape, 0)
    s = 1
    while s < n:
        x = x + jnp.where(row >= s, pltpu.roll(x, s, 0), 0.0)
        s *= 2
    o_ref[...] = x


def seq_cumsum(x, *, seq):
    m, d = x.shape
    return pl.pallas_call(
        _cumsum_kernel,
        grid=(m // seq,),
        in_specs=[pl.BlockSpec((seq, d), lambda b: (b, 0))],
        out_specs=pl.BlockSpec((seq, d), lambda b: (b, 0)),
        out_shape=jax.ShapeDtypeStruct((m, d), F32),
        compiler_params=_cparams("parallel"),
        name="seq_cumsum",
    )(x)


def _pair_cols(mat, pair, lane_lo):
    a = mat[:, 2 * pair:2 * pair + 1]
    b = mat[:, 2 * pair + 1:2 * pair + 2]
    return jnp.where(lane_lo, a, b)


def _ssd_kernel(xbc_ref, dtf_ref, zs_ref, cw_ref, cb_ref, a_ref, dskip_ref, gn_ref,
                o_ref, st_out_ref, xbuf, xc, st, y_scr, xw_scr):
    c = pl.program_id(1)
    nc = pl.num_programs(1)
    cs = SSD_CHUNK
    halo = SUBLANES

    @pl.when(c == 0)
    def _():
        xbuf[0:halo, :] = jnp.zeros((halo, SSD_CONV_DIM), F32)
        st[...] = jnp.zeros_like(st)

    @pl.when(c > 0)
    def _():
        xbuf[0:halo, :] = xbuf[cs:cs + halo, :]

    xbuf[halo:halo + cs, :] = xbc_ref[...]

    cw = cw_ref[...]
    cb = cb_ref[...]
    col_chunk = 512
    for j in range(SSD_CONV_DIM // col_chunk):
        sl = slice(j * col_chunk, (j + 1) * col_chunk)
        acc = xbuf[halo - 3:halo - 3 + cs, sl] * cw[0:1, sl]
        for i in range(1, SSD_CONV):
            acc = acc + xbuf[halo - 3 + i:halo - 3 + i + cs, sl] * cw[i:i + 1, sl]
        xc[:, sl] = _silu(acc + cb[:, sl])

    dt = dtf_ref[...]
    da = dt * a_ref[...]
    r_i = lax.broadcasted_iota(jnp.int32, (cs, cs), 0)
    c_i = lax.broadcasted_iota(jnp.int32, (cs, cs), 1)
    causal = c_i <= r_i
    tril = jnp.where(causal, 1.0, 0.0)
    a_cs = jnp.dot(tril, da, preferred_element_type=F32, precision=lax.Precision.HIGHEST)
    a_cs_t = a_cs.T
    dt_t = dt.T
    last = a_cs[cs - 1:cs, :]
    ecs = jnp.exp(a_cs)
    wfac = dt * jnp.exp(last - a_cs)
    cdec = jnp.exp(last)

    lane_lo = lax.broadcasted_iota(jnp.int32, (cs, LANES), 1) < SSD_HEAD_DIM
    lane_lo_row = lane_lo[0:1, :]
    pairs_per_group = HEADS_PER_GROUP // 2
    for g in range(SSD_GROUPS):
        b_off = SSD_INNER + g * SSD_STATE
        c_off = SSD_INNER + SSD_GROUPS * SSD_STATE + g * SSD_STATE
        bg = xc[:, b_off:b_off + SSD_STATE]
        cg = xc[:, c_off:c_off + SSD_STATE].astype(BF16)
        cbm = _dot_nt(cg, bg.astype(BF16))
        yoff = _dot(cg, st[g].astype(BF16))
        cd_tiles = []
        for pr in range(pairs_per_group):
            pair = g * pairs_per_group + pr
            col0 = pair * LANES
            xs_pair = xc[:, col0:col0 + LANES]
            xs_bf = xs_pair.astype(BF16)
            yd = []
            for h in (2 * pair, 2 * pair + 1):
                seg = a_cs[:, h:h + 1] - a_cs_t[h:h + 1, :]
                decay = jnp.exp(jnp.where(causal, seg, -jnp.inf))
                mh = (cbm * decay * dt_t[h:h + 1, :]).astype(BF16)
                yd.append(_dot(mh, xs_bf))
            y_pair = jnp.where(lane_lo, yd[0], yd[1])
            y_pair = y_pair + _pair_cols(ecs, pair, lane_lo) * yoff[:, pr * LANES:(pr + 1) * LANES]
            y_pair = y_pair + xs_pair * dskip_ref[:, col0:col0 + LANES]
            y_scr[:, col0:col0 + LANES] = y_pair
            xw_scr[:, pr * LANES:(pr + 1) * LANES] = (xs_pair * _pair_cols(wfac, pair, lane_lo)).astype(BF16)
            cd_tiles.append(_pair_cols(cdec, pair, lane_lo_row))
        cd_row = jnp.concatenate(cd_tiles, axis=1)
        st[g] = st[g] * cd_row + _dot(bg.T.astype(BF16), xw_scr[...])

    v = y_scr[...] * zs_ref[...]
    ms = jnp.mean(v * v, axis=-1, keepdims=True)
    o_ref[...] = (v * lax.rsqrt(ms + EPS) * gn_ref[...]).astype(o_ref.dtype)

    @pl.when(c == nc - 1)
    def _():
        for g in range(SSD_GROUPS):
            st_out_ref[0, g * GROUP_WIDTH:(g + 1) * GROUP_WIDTH, :] = st[g].T


def ssd_prompt(xbc, dtf, zs, conv_w, conv_b, a_row, dskip_row, gnorm, *, batch, seq):
    nc = seq // SSD_CHUNK
    cs = SSD_CHUNK
    row = lambda b, c: (b * nc + c, 0)
    const = lambda b, c: (0, 0)
    return pl.pallas_call(
        _ssd_kernel,
        grid=(batch, nc),
        in_specs=[
            pl.BlockSpec((cs, SSD_CONV_DIM), row),
            pl.BlockSpec((cs, DTF_WIDTH), row),
            pl.BlockSpec((cs, SSD_INNER), row),
            pl.BlockSpec((SSD_CONV, SSD_CONV_DIM), const),
            pl.BlockSpec((1, SSD_CONV_DIM), const),
            pl.BlockSpec((1, DTF_WIDTH), const),
            pl.BlockSpec((1, SSD_INNER), const),
            pl.BlockSpec((1, SSD_INNER), const),
        ],
        out_specs=[
            pl.BlockSpec((cs, SSD_INNER), row),
            pl.BlockSpec((1, SSD_HEADS * SSD_HEAD_DIM, SSD_STATE), lambda b, c: (b, 0, 0)),
        ],
        out_shape=[
            jax.ShapeDtypeStruct((batch * seq, SSD_INNER), BF16),
            jax.ShapeDtypeStruct((batch, SSD_HEADS * SSD_HEAD_DIM, SSD_STATE), F32),
        ],
        scratch_shapes=[
            pltpu.VMEM((cs + SUBLANES, SSD_CONV_DIM), F32),
            pltpu.VMEM((cs, SSD_CONV_DIM), F32),
            pltpu.VMEM((SSD_GROUPS, SSD_STATE, GROUP_WIDTH), F32),
            pltpu.VMEM((cs, SSD_INNER), F32),
            pltpu.VMEM((cs, GROUP_WIDTH), BF16),
        ],
        compiler_params=_cparams("parallel", "arbitrary"),
        name="ssd_prompt",
    )(xbc, dtf, zs, conv_w, conv_b, a_row, dskip_row, gnorm)


def _fox_kernel(q_ref, k_ref, v_ref, cq_ref, ck_ref, o_ref, *, seq, row_blocks, heads, scale):
    dh = FOX_HEAD_DIM
    blk = seq // row_blocks
    for h in range(heads):
        cols = slice(h * dh, (h + 1) * dh)
        for r in range(row_blocks):
            r0, kw = r * blk, (r + 1) * blk
            s = _dot_nt(q_ref[r0:r0 + blk, cols], k_ref[0:kw, cols]) * scale
            s = s + cq_ref[0, h, r0:r0 + blk, :] - ck_ref[0, h, :, 0:kw]
            r_i = lax.broadcasted_iota(jnp.int32, (blk, kw), 0) + r0
            c_i = lax.broadcasted_iota(jnp.int32, (blk, kw), 1)
            s = jnp.where(c_i <= r_i, s, -jnp.inf)
            p = jnp.exp(s - jnp.max(s, axis=-1, keepdims=True))
            l = jnp.sum(p, axis=-1, keepdims=True)
            o_ref[r0:r0 + blk, cols] = (_dot(p.astype(BF16), v_ref[0:kw, cols]) / l).astype(o_ref.dtype)


def fox_prompt(q, k, v, cq, ck, *, batch, seq, row_blocks, heads):
    width = heads * FOX_HEAD_DIM
    tokens = pl.BlockSpec((seq, width), lambda b, h: (b, h))
    return pl.pallas_call(
        functools.partial(_fox_kernel, seq=seq, row_blocks=row_blocks, heads=heads, scale=FOX_HEAD_DIM ** -0.5),
        grid=(batch, FOX_HEADS // heads),
        in_specs=[tokens, tokens, tokens,
                  pl.BlockSpec((1, heads, seq, 1), lambda b, h: (b, h, 0, 0)),
                  pl.BlockSpec((1, heads, 1, seq), lambda b, h: (b, h, 0, 0))],
        out_specs=tokens,
        out_shape=jax.ShapeDtypeStruct((batch * seq, FOX_WIDTH), BF16),
        compiler_params=_cparams("parallel", "arbitrary"),
        name="fox_prompt",
    )(q, k, v, cq, ck)


def _mem_kernel(q_ref, k_ref, v_ref, o_ref, *, scale):
    dh = MEM_HEAD_DIM
    for h in range(MEM_HEADS):
        cols = slice(h * dh, (h + 1) * dh)
        s = _dot_nt(q_ref[:, cols], k_ref[:, cols]) * scale
        e = jnp.exp(s - jnp.max(s, axis=-1, keepdims=True))
        p = e / jnp.sum(e, axis=-1, keepdims=True)
        o_ref[:, cols] = _dot(p.astype(BF16), v_ref[:, cols]).astype(o_ref.dtype)


def mem_prompt_attend(q, k, v, *, batch, seq, n_mem, tq):
    nq = seq // tq
    return pl.pallas_call(
        functools.partial(_mem_kernel, scale=MEM_HEAD_DIM ** -0.5),
        grid=(batch, nq),
        in_specs=[
            pl.BlockSpec((tq, MEM_WIDTH), lambda b, i: (b * nq + i, 0)),
            pl.BlockSpec((n_mem, MEM_WIDTH), lambda b, i: (b, 0)),
            pl.BlockSpec((n_mem, MEM_WIDTH), lambda b, i: (b, 0)),
        ],
        out_specs=pl.BlockSpec((tq, MEM_WIDTH), lambda b, i: (b * nq + i, 0)),
        out_shape=jax.ShapeDtypeStruct((batch * seq, MEM_WIDTH), BF16),
        compiler_params=_cparams("parallel", "arbitrary"),
        name="mem_prompt_attend",
    )(q, k, v)


def _merge_kernel(os_ref, of_ref, om_ref, ws_ref, wf_ref, wm_ref, g0_ref, g1_ref, g2_ref, o_ref):
    u = g0_ref[...] * _dot(os_ref[...], ws_ref[...].astype(BF16))
    u = u + g1_ref[...] * _dot(of_ref[...], wf_ref[...].astype(BF16))
    u = u + g2_ref[...] * _dot(om_ref[...], wm_ref[...].astype(BF16))
    o_ref[...] = u.astype(o_ref.dtype)


def branch_merge(o_ssd, o_fox, o_mem, w_branch, gates, *, tm, tn):
    m = o_ssd.shape[0]
    tm = min(tm, m)
    nj = D_MODEL // tn
    act = lambda k: pl.BlockSpec((tm, k), lambda i, j: (i, 0))
    gate = lambda br: pl.BlockSpec((tm, tn), lambda i, j: (i, br * nj + j))

    def wgt(row0, k):
        assert row0 % k == 0
        return pl.BlockSpec((None, k, tn), lambda i, j: (0, row0 // k, j))

    return pl.pallas_call(
        _merge_kernel,
        grid=(m // tm, nj),
        in_specs=[act(SSD_INNER), act(FOX_WIDTH), act(MEM_WIDTH),
                  wgt(0, SSD_INNER), wgt(SSD_INNER, FOX_WIDTH), wgt(SSD_INNER + FOX_WIDTH, MEM_WIDTH),
                  gate(0), gate(1), gate(2)],
        out_specs=pl.BlockSpec((tm, tn), lambda i, j: (i, j)),
        out_shape=jax.ShapeDtypeStruct((m, D_MODEL), BF16),
        compiler_params=_cparams("parallel", "arbitrary"),
        name="branch_merge",
    )(o_ssd, o_fox, o_mem, w_branch, w_branch, w_branch, gates, gates, gates)


def _outproj_kernel(a_ref, w_ref, x_ref, g_ref, h_ref, hn_ref):
    h = x_ref[...] + _dot(a_ref[...], w_ref[...])
    h_ref[...] = h
    ms = jnp.mean(h * h, axis=-1, keepdims=True)
    hn_ref[...] = (h * lax.rsqrt(ms + EPS) * g_ref[...]).astype(hn_ref.dtype)


def out_proj(merged, w_out, x, g_ffn, *, tm):
    m = x.shape[0]
    tm = min(tm, m)
    row = lambda i: (i, 0)
    const = lambda i: (0, 0)
    return pl.pallas_call(
        _outproj_kernel,
        grid=(m // tm,),
        in_specs=[pl.BlockSpec((tm, D_MODEL), row), pl.BlockSpec((D_MODEL, D_MODEL), const),
                  pl.BlockSpec((tm, D_MODEL), row), pl.BlockSpec((1, D_MODEL), const)],
        out_specs=[pl.BlockSpec((tm, D_MODEL), row), pl.BlockSpec((tm, D_MODEL), row)],
        out_shape=[jax.ShapeDtypeStruct((m, D_MODEL), F32), jax.ShapeDtypeStruct((m, D_MODEL), BF16)],
        compiler_params=_cparams("parallel"),
        name="out_proj",
    )(merged, w_out, x, g_ffn.reshape(1, D_MODEL))


def _ffn_kernel(hn_ref, wg_ref, wu_ref, cw_ref, cb_ref, act_ref, buf_ref, gbuf, *, tiles_per_seq):
    i = pl.program_id(1)
    tm = hn_ref.shape[0]
    halo = SUBLANES

    @pl.when(i % tiles_per_seq == 0)
    def _():
        gbuf[0:halo, :] = jnp.zeros((halo, gbuf.shape[1]), F32)

    @pl.when(i % tiles_per_seq != 0)
    def _():
        gbuf[0:halo, :] = gbuf[tm:tm + halo, :]

    hn = hn_ref[...]
    gbuf[halo:halo + tm, :] = _dot(hn, wg_ref[...].astype(BF16))
    cw = cw_ref[...]
    conv = gbuf[halo - 2:halo - 2 + tm, :] * cw[0:1, :]
    conv = conv + gbuf[halo - 1:halo - 1 + tm, :] * cw[1:2, :]
    conv = conv + gbuf[halo:halo + tm, :] * cw[2:3, :]
    conv = conv + cb_ref[...]
    act_ref[...] = (_silu(conv) * _dot(hn, wu_ref[...].astype(BF16))).astype(act_ref.dtype)
    buf_ref[0] = gbuf[halo + tm - 2:halo + tm, :]


def ffn_gate_up(hn, w_gate, w_up, conv_w, conv_b, *, batch, seq, tm, tn):
    m = hn.shape[0]
    tiles_per_seq = seq // tm
    return pl.pallas_call(
        functools.partial(_ffn_kernel, tiles_per_seq=tiles_per_seq),
        grid=(D_FF // tn, m // tm),
        in_specs=[
            pl.BlockSpec((tm, D_MODEL), lambda j, i: (i, 0)),
            pl.BlockSpec((None, D_MODEL, tn), lambda j, i: (0, 0, j)),
            pl.BlockSpec((None, D_MODEL, tn), lambda j, i: (0, 0, j)),
            pl.BlockSpec((FFN_CONV, tn), lambda j, i: (0, j)),
            pl.BlockSpec((1, tn), lambda j, i: (0, j)),
        ],
        out_specs=[
            pl.BlockSpec((tm, tn), lambda j, i: (i, j)),
            pl.BlockSpec((1, FFN_CONV - 1, tn), lambda j, i: (i // tiles_per_seq, 0, j)),
        ],
        out_shape=[
            jax.ShapeDtypeStruct((m, D_FF), BF16),
            jax.ShapeDtypeStruct((batch, FFN_CONV - 1, D_FF), F32),
        ],
        scratch_shapes=[pltpu.VMEM((tm + SUBLANES, tn), F32)],
        compiler_params=_cparams("parallel", "arbitrary"),
        name="ffn_gate_up",
    )(hn, w_gate, w_up, conv_w, conv_b)


def _down_kernel(a_ref, w_ref, h_ref, g_ref, y_ref, acc):
    k = pl.program_id(1)

    @pl.when(k == 0)
    def _():
        acc[...] = h_ref[...]

    acc[...] += _dot(a_ref[...], w_ref[...].astype(BF16))

    @pl.when(k == pl.num_programs(1) - 1)
    def _():
        h = acc[...]
        ms = jnp.mean(h * h, axis=-1, keepdims=True)
        y_ref[...] = h * lax.rsqrt(ms + EPS) * g_ref[...]


def ffn_down(act, w_down, h, g_final, *, tm, tk):
    m = h.shape[0]
    tm = min(tm, m)
    return pl.pallas_call(
        _down_kernel,
        grid=(m // tm, D_FF // tk),
        in_specs=[pl.BlockSpec((tm, tk), lambda i, k: (i, k)), pl.BlockSpec((None, tk, D_MODEL), lambda i, k: (0, k, 0)),
                  pl.BlockSpec((tm, D_MODEL), lambda i, k: (i, 0)), pl.BlockSpec((1, D_MODEL), lambda i, k: (0, 0))],
        out_specs=pl.BlockSpec((tm, D_MODEL), lambda i, k: (i, 0)),
        out_shape=jax.ShapeDtypeStruct((m, D_MODEL), F32),
        scratch_shapes=[pltpu.VMEM((tm, D_MODEL), F32)],
        compiler_params=_cparams("parallel", "arbitrary"),
        name="ffn_down",
    )(act, w_down, h, g_final.reshape(1, D_MODEL))


def _ssd_s_prep_kernel(xbc_ref, buf_ref, cw_ref, cb_ref, dtf_ref, a_ref, expand_ref, xc_ref, cols_ref):
    cw = cw_ref[...]
    acc = buf_ref[0] * cw[0:1, :]
    acc = acc + buf_ref[1] * cw[1:2, :]
    acc = acc + buf_ref[2] * cw[2:3, :]
    acc = acc + xbc_ref[...] * cw[3:4, :]
    xcv = _silu(acc + cb_ref[...])
    xc_ref[...] = xcv
    dt = dtf_ref[...]
    decay = jnp.exp(dt * a_ref[...])
    nb = dt.shape[0]
    parts = _split3(dt) + _split3(decay)
    stacked = jnp.concatenate(parts, axis=0)
    ex = _dot(stacked, expand_ref[...])
    dt_exp = ex[0:nb] + ex[nb:2 * nb] + ex[2 * nb:3 * nb]
    dec_exp = ex[3 * nb:4 * nb] + ex[4 * nb:5 * nb] + ex[5 * nb:6 * nb]
    xdt = xcv[:, 0:SSD_INNER] * dt_exp
    pad = jnp.zeros((LANES - 2 * nb, SSD_INNER), F32)
    cols_ref[...] = jnp.concatenate([xdt, dec_exp, pad], axis=0).T


def ssd_sample_prep(xbc, buf_t, conv_w, conv_b, dtf, a_row, expand):
    nb = xbc.shape[0]
    full = lambda s: pl.BlockSpec(s, lambda i: (0,) * len(s))
    return pl.pallas_call(
        _ssd_s_prep_kernel,
        grid=(1,),
        in_specs=[full(xbc.shape), full(buf_t.shape), full(conv_w.shape), full(conv_b.shape), full(dtf.shape),
                  full(a_row.shape), full(expand.shape)],
        out_specs=[full((nb, SSD_CONV_DIM)), full((SSD_INNER, LANES))],
        out_shape=[jax.ShapeDtypeStruct((nb, SSD_CONV_DIM), F32), jax.ShapeDtypeStruct((SSD_INNER, LANES), F32)],
        compiler_params=_cparams("arbitrary"),
        name="ssd_sample_prep",
    )(xbc, buf_t, conv_w, conv_b, dtf, a_row, expand)


def _ssd_s_step_kernel(cols_ref, h0_ref, xc_ref, zs_ref, dskip_ref, gn_ref, hn_ref, o_ref, *, nb):
    b = pl.program_id(0)
    cols = cols_ref[...]
    hi, mid, lo = _split3(cols)
    lhs = jnp.concatenate([hi, mid, lo], axis=1)
    r_i = lax.broadcasted_iota(jnp.int32, (3 * LANES, 2 * LANES), 0) % LANES
    c_i = lax.broadcasted_iota(jnp.int32, (3 * LANES, 2 * LANES), 1)
    sel = jnp.where(r_i == jnp.where(c_i < LANES, b, nb + b), 1.0, 0.0).astype(BF16)
    picked = _dot(lhs, sel)
    xb = picked[:, 0:LANES]
    db = picked[:, LANES:2 * LANES]
    xc = xc_ref[0]
    y_parts = []
    for g in range(SSD_GROUPS):
        rows = slice(g * GROUP_WIDTH, (g + 1) * GROUP_WIDTH)
        b_row = xc[:, SSD_INNER + g * SSD_STATE:SSD_INNER + (g + 1) * SSD_STATE]
        c_off = SSD_INNER + SSD_GROUPS * SSD_STATE + g * SSD_STATE
        c_row = xc[:, c_off:c_off + SSD_STATE]
        hn = h0_ref[0, rows, :] * db[rows, :] + xb[rows, :] * b_row
        hn_ref[0, rows, :] = hn
        c8 = jnp.broadcast_to(c_row, (SUBLANES, SSD_STATE)).astype(BF16)
        y_parts.append(_dot_nt(c8, hn.astype(BF16))[0:1, :])
    xs = xc[:, 0:SSD_INNER]
    y = jnp.concatenate(y_parts, axis=1) + xs * dskip_ref[...]
    v = y * zs_ref[0]
    ms = jnp.mean(v * v, axis=-1, keepdims=True)
    o_ref[0] = (v * lax.rsqrt(ms + EPS) * gn_ref[...]).astype(o_ref.dtype)


def ssd_sample_step(cols, h0, xc, zs, dskip_row, gnorm):
    nb = h0.shape[0]
    rows = SSD_HEADS * SSD_HEAD_DIM
    const2 = lambda b: (0, 0)
    per_b = lambda b: (b, 0, 0)
    return pl.pallas_call(
        functools.partial(_ssd_s_step_kernel, nb=nb),
        grid=(nb,),
        in_specs=[
            pl.BlockSpec((SSD_INNER, LANES), const2),
            pl.BlockSpec((1, rows, SSD_STATE), per_b),
            pl.BlockSpec((1, 1, SSD_CONV_DIM), per_b),
            pl.BlockSpec((1, 1, SSD_INNER), per_b),
            pl.BlockSpec((1, SSD_INNER), const2),
            pl.BlockSpec((1, SSD_INNER), const2),
        ],
        out_specs=[pl.BlockSpec((1, rows, SSD_STATE), per_b), pl.BlockSpec((1, 1, SSD_INNER), per_b)],
        out_shape=[jax.ShapeDtypeStruct((nb, rows, SSD_STATE), F32), jax.ShapeDtypeStruct((nb, 1, SSD_INNER), BF16)],
        compiler_params=_cparams("parallel"),
        name="ssd_sample_step",
    )(cols, h0, xc.reshape(nb, 1, SSD_CONV_DIM), zs.reshape(nb, 1, SSD_INNER), dskip_row, gnorm)


def _head_matched(shape, n_heads):
    r_i = lax.broadcasted_iota(jnp.int32, shape, 0)
    c_i = lax.broadcasted_iota(jnp.int32, shape, 1)
    return c_i % n_heads == r_i % n_heads


def _fox_s_kernel(pt_ref, *refs, n_steps, scale):
    del pt_ref
    pp = PAGES_PER_STEP
    nh = FOX_HEADS
    q_ref, kn_ref, vn_ref, lfn_ref, suf_ref = refs[0:5]
    k_refs = refs[5:5 + pp]
    v_refs = refs[5 + pp:5 + 2 * pp]
    lf_refs = refs[5 + 2 * pp:5 + 3 * pp]
    o_ref = refs[5 + 3 * pp]
    m_scr, l_scr, acc_scr, carry = refs[6 + 3 * pp:]
    j = pl.program_id(1)

    @pl.when(j == 0)
    def _():
        m_scr[...] = jnp.full_like(m_scr, -jnp.inf)
        l_scr[...] = jnp.zeros_like(l_scr)
        acc_scr[...] = jnp.zeros_like(acc_scr)
        carry[...] = lfn_ref[0]

    q = q_ref[0].astype(BF16)
    page = k_refs[0].shape[2]
    rows = page * nh // LANES
    keep = _head_matched((nh, page * nh), nh)
    row_i = lax.broadcasted_iota(jnp.int32, (rows, LANES), 0)

    run = carry[...]
    scores = []
    lfd_all = jnp.concatenate([lf_refs[i][0] for i in range(pp)], axis=0)
    r_all = sum(_dot(part, suf_ref[...]) for part in _split3(lfd_all))
    for i in range(pp):
        r = r_all[i * rows:(i + 1) * rows]
        within = r[:, 0:LANES]
        rowtot = r[:, LANES:]
        incl = rowtot
        sft = 1
        while sft < rows:
            incl = incl + jnp.where(row_i + sft < rows, pltpu.roll(incl, rows - sft, 0), 0.0)
            sft *= 2
        bias_d = within + (incl - rowtot) + run
        run = run + incl[0:1, :]
        bias = jnp.concatenate([jnp.broadcast_to(bias_d[t:t + 1, :], (nh, LANES)) for t in range(rows)], axis=1)
        k2 = k_refs[i][0, 0].reshape(page * nh, FOX_HEAD_DIM).astype(BF16)
        scores.append(jnp.where(keep, _dot_nt(q, k2) * scale + bias, -jnp.inf))
    carry[...] = run

    m_old = m_scr[...]
    m_new = m_old
    for s in scores:
        m_new = jnp.maximum(m_new, jnp.max(s, axis=-1, keepdims=True))
    alpha = jnp.exp(m_old - m_new)
    l = alpha * l_scr[...]
    acc = alpha * acc_scr[...]
    for i, s in enumerate(scores):
        p = jnp.exp(s - m_new)
        l = l + jnp.sum(p, axis=-1, keepdims=True)
        v2 = v_refs[i][0, 0].reshape(page * nh, FOX_HEAD_DIM).astype(BF16)
        acc = acc + _dot(p.astype(BF16), v2)
    m_scr[...] = m_new
    l_scr[...] = l
    acc_scr[...] = acc

    @pl.when(j == n_steps - 1)
    def _():
        s_self = jnp.sum(q_ref[0] * kn_ref[0], axis=-1, keepdims=True) * scale
        m_old = m_scr[...]
        m_new = jnp.maximum(m_old, s_self)
        alpha = jnp.exp(m_old - m_new)
        p_self = jnp.exp(s_self - m_new)
        l = alpha * l_scr[...] + p_self
        acc = alpha * acc_scr[...] + p_self * vn_ref[0]
        o_ref[0] = (acc / l).astype(o_ref.dtype)


def fox_sample(page_table, q, k_new, v_new, lf_new_tiled, k_pool, v_pool, lf_pool_dense):
    nb, n_pages = page_table.shape
    page = k_pool.shape[2]
    rows = lf_pool_dense.shape[1]
    pp = PAGES_PER_STEP
    n_steps = n_pages // pp
    r_i = lax.broadcasted_iota(jnp.int32, (LANES, 2 * LANES), 0)
    c_i = lax.broadcasted_iota(jnp.int32, (LANES, 2 * LANES), 1)
    same = r_i % FOX_HEADS == c_i % FOX_HEADS
    suf = (same & ((c_i >= LANES) | (r_i > c_i))).astype(BF16)

    def newest_first(i):
        return lambda b, j, pt: pt[b, n_pages - 1 - (j * pp + i)]

    def kv_spec(i):
        pick = newest_first(i)
        return pl.BlockSpec((1, 1, page, FOX_HEADS, FOX_HEAD_DIM), lambda b, j, pt: (0, pick(b, j, pt), 0, 0, 0))

    def lf_spec(i):
        pick = newest_first(i)
        return pl.BlockSpec((1, rows, LANES), lambda b, j, pt: (pick(b, j, pt), 0, 0))

    per_b = lambda shape: pl.BlockSpec((1,) + shape, lambda b, j, pt: (b, 0, 0))
    hd = (FOX_HEADS, FOX_HEAD_DIM)
    in_specs = [per_b(hd), per_b(hd), per_b(hd), per_b((1, LANES)),
                pl.BlockSpec((LANES, 2 * LANES), lambda b, j, pt: (0, 0))]
    in_specs += [kv_spec(i) for i in range(pp)]
    in_specs += [kv_spec(i) for i in range(pp)]
    in_specs += [lf_spec(i) for i in range(pp)]
    grid_spec = pltpu.PrefetchScalarGridSpec(
        num_scalar_prefetch=1,
        grid=(nb, n_steps),
        in_specs=in_specs,
        out_specs=per_b(hd),
        scratch_shapes=[
            pltpu.VMEM((FOX_HEADS, 1), F32),
            pltpu.VMEM((FOX_HEADS, 1), F32),
            pltpu.VMEM(hd, F32),
            pltpu.VMEM((1, LANES), F32),
        ],
    )
    return pl.pallas_call(
        functools.partial(_fox_s_kernel, n_steps=n_steps, scale=FOX_HEAD_DIM ** -0.5),
        grid_spec=grid_spec,
        out_shape=jax.ShapeDtypeStruct((nb,) + hd, BF16),
        compiler_params=_cparams("parallel", "arbitrary"),
        name="fox_sample",
    )(page_table, q.reshape((nb,) + hd), k_new.reshape((nb,) + hd), v_new.reshape((nb,) + hd),
      lf_new_tiled.reshape(nb, 1, LANES), suf,
      *([k_pool] * pp), *([v_pool] * pp), *([lf_pool_dense] * pp))


def _mem_s_kernel(q_ref, k_ref, v_ref, o_ref, *, scale):
    n_mem = k_ref.shape[2]
    k2 = k_ref[0, 0].reshape(n_mem * MEM_HEADS, MEM_HEAD_DIM).astype(BF16)
    v2 = v_ref[0, 0].reshape(n_mem * MEM_HEADS, MEM_HEAD_DIM).astype(BF16)
    q = q_ref[0]
    keep = _head_matched((q.shape[0], n_mem * MEM_HEADS), MEM_HEADS)
    s = jnp.where(keep, _dot_nt(q, k2) * scale, -jnp.inf)
    e = jnp.exp(s - jnp.max(s, axis=-1, keepdims=True))
    p = e / jnp.sum(e, axis=-1, keepdims=True)
    o_ref[0] = _dot(p.astype(BF16), v2).astype(o_ref.dtype)


def mem_sample_attend(q_rep, k, v):
    _, nb, n_mem, _, _ = k.shape
    rep = q_rep.shape[1]
    kv = pl.BlockSpec((1, 1, n_mem, MEM_HEADS, MEM_HEAD_DIM), lambda b: (0, b, 0, 0, 0))
    return pl.pallas_call(
        functools.partial(_mem_s_kernel, scale=MEM_HEAD_DIM ** -0.5),
        grid=(nb,),
        in_specs=[pl.BlockSpec((1, rep, MEM_HEAD_DIM), lambda b: (b, 0, 0)), kv, kv],
        out_specs=pl.BlockSpec((1, rep, MEM_HEAD_DIM), lambda b: (b, 0, 0)),
        out_shape=jax.ShapeDtypeStruct((nb, rep, MEM_HEAD_DIM), BF16),
        compiler_params=_cparams("parallel"),
        name="mem_sample_attend",
    )(q_rep, k, v)


def _ffn_s_kernel(g_ref, u_ref, buf_ref, cw_ref, cb_ref, o_ref):
    cw = cw_ref[...]
    conv = buf_ref[0] * cw[0:1, :]
    conv = conv + buf_ref[1] * cw[1:2, :]
    conv = conv + g_ref[...] * cw[2:3, :]
    conv = conv + cb_ref[...]
    o_ref[...] = (_silu(conv) * u_ref[...]).astype(o_ref.dtype)


def ffn_sample_act(g, u, buf_t, conv_w, conv_b):
    full = lambda s: pl.BlockSpec(s, lambda i: (0,) * len(s))
    return pl.pallas_call(
        _ffn_s_kernel,
        grid=(1,),
        in_specs=[full(g.shape), full(u.shape), full(buf_t.shape), full(conv_w.shape), full(conv_b.shape)],
        out_specs=full(g.shape),
        out_shape=jax.ShapeDtypeStruct(g.shape, BF16),
        compiler_params=_cparams("arbitrary"),
        name="ffn_sample_act",
    )(g, u, buf_t, conv_w, conv_b)


def _project(xn, w, *, tm, tn):
    mm = functools.partial(matmul, xn, tm=tm)
    (zs,) = mm(w["w_in"], cols=(0, SSD_INNER), tn=tn, out_dtypes=(F32,), epilogue=lambda acc: (_silu(acc),),
               name="proj_z")
    (xbc,) = mm(w["w_in"], cols=(SSD_INNER, SSD_CONV_DIM), tn=tn, out_dtypes=(F32,), name="proj_xbc")
    (dtf,) = mm(w["dtf"], tn=DTF_WIDTH, out_dtypes=(F32,), epilogue=_dtf_epilogue, rows=(w["dtf_bias"],),
                name="proj_dtf")
    (q,) = mm(w["q"], tn=tn, out_dtypes=(BF16,), name="proj_q")
    k, k_bf = mm(w["k"], tn=tn, out_dtypes=(F32, BF16), epilogue=lambda acc: (acc, acc), name="proj_k")
    v, v_bf = mm(w["v"], tn=tn, out_dtypes=(F32, BF16), epilogue=lambda acc: (acc, acc), name="proj_v")
    (mq,) = mm(w["mq"], tn=tn, out_dtypes=(BF16,), name="proj_mq")
    (gates,) = mm(w["gates"], tn=tn, out_dtypes=(F32,), epilogue=lambda acc: (jax.nn.sigmoid(acc),),
                  name="proj_gates")
    return dict(zs=zs, xbc=xbc, dtf=dtf, q=q, k=k, k_bf=k_bf, v=v, v_bf=v_bf, mq=mq, gates=gates)


def kernel(x_prompt, x_sample, cache_fox_k, cache_fox_v, cache_fox_logf, cache_mem_k, cache_mem_v, state_ssd,
           state_ssd_conv, state_ffn_conv, page_table, mem_prompt, g_mix, w_in, ssd_conv_w, ssd_conv_b,
           ssd_dt_bias, ssd_a_log, ssd_d, ssd_g_norm, fox_b_forget, g_mem, w_mem_kv, w_branch, w_out, g_ffn,
           w_ffn_gate, w_ffn_up, ffn_conv_w, ffn_conv_b, w_ffn_down, g_final):
    depth = w_in.shape[0]
    assert depth == 1, "single-layer trunk"
    bp, seq, d = x_prompt.shape
    nb, dec_seq, _ = x_sample.shape
    assert d == D_MODEL and dec_seq == 1 and seq % SSD_CHUNK == 0
    n_mem = mem_prompt.shape[1]
    n_pool, page = cache_fox_k.shape[1], cache_fox_k.shape[2]
    assert page_table.shape[1] % PAGES_PER_STEP == 0
    mp = bp * seq

    wi = w_in[0]
    offs = [0]
    for width in (SSD_INNER, SSD_CONV_DIM, SSD_HEADS, FOX_WIDTH, FOX_WIDTH, FOX_WIDTH, FOX_HEADS, MEM_WIDTH,
                  N_BRANCH * D_MODEL):
        offs.append(offs[-1] + width)
    assert offs[-1] == wi.shape[1]
    seg = lambda i: wi[:, offs[i]:offs[i + 1]].astype(BF16)
    dtf_pad = DTF_WIDTH - SSD_HEADS - FOX_HEADS
    w = dict(
        w_in=w_in, q=seg(3), k=seg(4), v=seg(5), mq=seg(7), gates=seg(8),
        dtf=jnp.concatenate([seg(2), seg(6), jnp.zeros((D_MODEL, dtf_pad), BF16)], axis=1),
        dtf_bias=jnp.concatenate([ssd_dt_bias[0], fox_b_forget[0], jnp.zeros((dtf_pad,), F32)]).reshape(1, DTF_WIDTH),
    )
    w_o = w_out[0].astype(BF16)
    w_mk = w_mem_kv[0][:, :MEM_WIDTH].astype(BF16)
    w_mv = w_mem_kv[0][:, MEM_WIDTH:].astype(BF16)
    a_row = jnp.concatenate([-jnp.exp(ssd_a_log[0]), jnp.zeros((DTF_WIDTH - SSD_HEADS,), F32)]).reshape(1, DTF_WIDTH)
    dskip_row = jnp.repeat(ssd_d[0], SSD_HEAD_DIM).reshape(1, SSD_INNER)
    gnorm_row = ssd_g_norm[0].reshape(1, SSD_INNER)
    conv_w = ssd_conv_w[0]
    conv_b = ssd_conv_b[0].reshape(1, SSD_CONV_DIM)
    fconv_w = ffn_conv_w[0]
    fconv_b = ffn_conv_b[0].reshape(1, D_FF)

    xp = x_prompt.reshape(mp, D_MODEL)
    pr = _project(rms_norm(xp, g_mix[0], tm=512), w, tm=2048, tn=512)

    mem_n = rms_norm(mem_prompt.reshape(bp * n_mem, D_MODEL), g_mem[0], tm=512)
    mk_p, mk_bf = matmul(mem_n, w_mk, tm=1024, tn=512, out_dtypes=(F32, BF16), epilogue=lambda acc: (acc, acc),
                         name="proj_mem_k")
    mv_p, mv_bf = matmul(mem_n, w_mv, tm=1024, tn=512, out_dtypes=(F32, BF16), epilogue=lambda acc: (acc, acc),
                         name="proj_mem_v")

    o_ssd_p, st_p = ssd_prompt(pr["xbc"], pr["dtf"], pr["zs"], conv_w, conv_b, a_row, dskip_row, gnorm_row,
                               batch=bp, seq=seq)

    c_all = seq_cumsum(pr["dtf"], seq=seq)
    c_p = c_all[:, SSD_HEADS:SSD_HEADS + FOX_HEADS].reshape(bp, seq, FOX_HEADS).transpose(0, 2, 1)
    o_fox_p = fox_prompt(pr["q"], pr["k_bf"], pr["v_bf"], c_p.reshape(bp, FOX_HEADS, seq, 1),
                         c_p.reshape(bp, FOX_HEADS, 1, seq), batch=bp, seq=seq, row_blocks=4, heads=4)
    o_mem_p = mem_prompt_attend(pr["mq"], mk_bf, mv_bf, batch=bp, seq=seq, n_mem=n_mem, tq=1024)

    merged_p = branch_merge(o_ssd_p, o_fox_p, o_mem_p, w_branch, pr["gates"], tm=1024, tn=256)
    h_p, hn_p = out_proj(merged_p, w_o, xp, g_ffn[0], tm=512)
    act_p, fbuf_p = ffn_gate_up(hn_p, w_ffn_gate, w_ffn_up, fconv_w, fconv_b, batch=bp, seq=seq, tm=1024, tn=512)
    y_p = ffn_down(act_p, w_ffn_down, h_p, g_final, tm=512, tk=D_FF // 4)

    xs_ = x_sample.reshape(nb, D_MODEL)
    sm = _project(rms_norm(xs_, g_mix[0], tm=nb), w, tm=nb, tn=2048)

    buf_t = state_ssd_conv[0].transpose(1, 0, 2)
    h_i = lax.broadcasted_iota(jnp.int32, (DTF_WIDTH, SSD_INNER), 0)
    col_i = lax.broadcasted_iota(jnp.int32, (DTF_WIDTH, SSD_INNER), 1)
    expand = (col_i // SSD_HEAD_DIM == h_i).astype(BF16)
    xc_s, cols_s = ssd_sample_prep(sm["xbc"], buf_t, conv_w, conv_b, sm["dtf"], a_row, expand)
    st_s, o_ssd_s = ssd_sample_step(cols_s, state_ssd[0].reshape(nb, SSD_HEADS * SSD_HEAD_DIM, SSD_STATE), xc_s,
                                    sm["zs"], dskip_row, gnorm_row)

    lf_s = sm["dtf"][:, SSD_HEADS:SSD_HEADS + FOX_HEADS]
    lf_dense = cache_fox_logf[0].reshape(n_pool, page * FOX_HEADS // LANES, LANES)
    o_fox_s = fox_sample(page_table, sm["q"].astype(F32), sm["k"], sm["v"], jnp.tile(lf_s, (1, LANES // FOX_HEADS)),
                         cache_fox_k, cache_fox_v, lf_dense)
    mq_rep = jnp.tile(sm["mq"].reshape(nb, MEM_HEADS, MEM_HEAD_DIM), (1, 2 * SUBLANES // MEM_HEADS, 1))
    o_mem_s = mem_sample_attend(mq_rep, cache_mem_k, cache_mem_v)
    o_mem_s = o_mem_s[:, :MEM_HEADS].reshape(nb, MEM_WIDTH)

    merged_s = branch_merge(o_ssd_s.reshape(nb, SSD_INNER), o_fox_s.reshape(nb, FOX_WIDTH), o_mem_s, w_branch,
                            sm["gates"], tm=nb, tn=512)
    h_s, hn_s = out_proj(merged_s, w_o, xs_, g_ffn[0], tm=nb)
    (g_s,) = matmul(hn_s, w_ffn_gate, cols=(0, D_FF), tm=nb, tn=D_FF // 4, out_dtypes=(F32,), name="ffn_gate_s")
    (u_s,) = matmul(hn_s, w_ffn_up, cols=(0, D_FF), tm=nb, tn=D_FF // 4, out_dtypes=(F32,), name="ffn_up_s")
    act_s = ffn_sample_act(g_s, u_s, state_ffn_conv[0].transpose(1, 0, 2), fconv_w, fconv_b)
    y_s = ffn_down(act_s, w_ffn_down, h_s, g_final, tm=nb, tk=D_FF // 4)

    lf_p = pr["dtf"][:, SSD_HEADS:SSD_HEADS + FOX_HEADS]
    heads5 = lambda t, b, l: t.reshape(1, b, l, FOX_HEADS, FOX_HEAD_DIM)
    return (
        y_p.reshape(bp, seq, D_MODEL),
        y_s.reshape(nb, 1, D_MODEL),
        heads5(pr["k"], bp, seq),
        heads5(pr["v"], bp, seq),
        lf_p.reshape(1, bp, seq, FOX_HEADS),
        mk_p.reshape(1, bp, n_mem, MEM_HEADS, MEM_HEAD_DIM),
        mv_p.reshape(1, bp, n_mem, MEM_HEADS, MEM_HEAD_DIM),
        st_p.reshape(1, bp, SSD_HEADS, SSD_HEAD_DIM, SSD_STATE),
        pr["xbc"].reshape(bp, seq, SSD_CONV_DIM)[:, seq - (SSD_CONV - 1):][None],
        fbuf_p[None],
        heads5(sm["k"], nb, 1),
        heads5(sm["v"], nb, 1),
        lf_s.reshape(1, nb, 1, FOX_HEADS),
        st_s.reshape(1, nb, SSD_HEADS, SSD_HEAD_DIM, SSD_STATE),
        jnp.concatenate([state_ssd_conv[0][:, 1:], sm["xbc"][:, None, :]], axis=1)[None],
        jnp.concatenate([state_ffn_conv[0][:, 1:], g_s[:, None, :]], axis=1)[None],
    )
```

```python
import functools

import jax
import jax.numpy as jnp
from jax import lax
from jax.experimental import pallas as pl
from jax.experimental.pallas import tpu as pltpu

F32 = jnp.float32
BF16 = jnp.bfloat16

D_MODEL = 2048
SSD_INNER = 4096
SSD_HEAD_DIM = 64
SSD_HEADS = 64
SSD_GROUPS = 8
SSD_STATE = 128
SSD_CONV = 4
SSD_CHUNK = 128
SSD_CONV_DIM = SSD_INNER + 2 * SSD_GROUPS * SSD_STATE
HEADS_PER_GROUP = SSD_HEADS // SSD_GROUPS
GROUP_WIDTH = HEADS_PER_GROUP * SSD_HEAD_DIM
FOX_HEAD_DIM = 128
FOX_HEADS = 16
FOX_WIDTH = FOX_HEADS * FOX_HEAD_DIM
MEM_HEADS = 4
MEM_HEAD_DIM = 128
MEM_WIDTH = MEM_HEADS * MEM_HEAD_DIM
D_FF = 5632
FFN_CONV = 3
N_BRANCH = 3
EPS = 1e-6

LANES = 128
SUBLANES = 8
VMEM_LIMIT_BYTES = 56 * 1024 * 1024

DTF_WIDTH = LANES
PAGES_PER_STEP = 8


def _cparams(*semantics):
    return pltpu.CompilerParams(dimension_semantics=semantics, vmem_limit_bytes=VMEM_LIMIT_BYTES)


def _dot(a, b):
    return jnp.dot(a, b, preferred_element_type=F32)


def _dot_nt(a, b):
    return lax.dot_general(a, b, (((1,), (1,)), ((), ())), preferred_element_type=F32)


def _silu(x):
    return x * jax.nn.sigmoid(x)


def _softplus(x):
    return jnp.maximum(x, 0.0) + jnp.log1p(jnp.exp(-jnp.abs(x)))


def _split3(x):
    hi = x.astype(BF16)
    r1 = x - hi.astype(F32)
    mid = r1.astype(BF16)
    lo = (r1 - mid.astype(F32)).astype(BF16)
    return hi, mid, lo


def _rms_kernel(x_ref, g_ref, o_ref):
    x = x_ref[...]
    ms = jnp.mean(x * x, axis=-1, keepdims=True)
    o_ref[...] = (x * lax.rsqrt(ms + EPS) * g_ref[...]).astype(o_ref.dtype)


def rms_norm(x, g, *, tm, out_dtype=BF16):
    m, d = x.shape
    return pl.pallas_call(
        _rms_kernel,
        grid=(m // tm,),
        in_specs=[pl.BlockSpec((tm, d), lambda i: (i, 0)), pl.BlockSpec((1, d), lambda i: (0, 0))],
        out_specs=pl.BlockSpec((tm, d), lambda i: (i, 0)),
        out_shape=jax.ShapeDtypeStruct((m, d), out_dtype),
        compiler_params=_cparams("parallel"),
        name="rms_norm",
    )(x, g.reshape(1, d))


def _mm_kernel(*refs, epilogue, n_rows):
    a_ref, w_ref = refs[0], refs[1]
    rows = [r[...] for r in refs[2:2 + n_rows]]
    outs = refs[2 + n_rows:]
    acc = _dot(a_ref[...], w_ref[...].astype(BF16))
    vals = epilogue(acc, *rows)
    for o, v in zip(outs, vals):
        o[...] = v.astype(o.dtype)


def matmul(a, w, *, tm, tn, out_dtypes, epilogue=lambda acc: (acc,), rows=(), name="matmul", cols=None):
    m, k = a.shape
    tm = min(tm, m)
    if cols is None:
        n = w.shape[1]
        tn = min(tn, n)
        w_spec = pl.BlockSpec((k, tn), lambda i, j: (0, j))
    else:
        start, n = cols
        tn = min(tn, n)
        assert start % tn == 0 and n % tn == 0
        first = start // tn
        w_spec = pl.BlockSpec((None, k, tn), lambda i, j: (0, 0, first + j))
    in_specs = [pl.BlockSpec((tm, k), lambda i, j: (i, 0)), w_spec]
    in_specs += [pl.BlockSpec((1, tn), lambda i, j: (0, j)) for _ in rows]
    outs = pl.pallas_call(
        functools.partial(_mm_kernel, epilogue=epilogue, n_rows=len(rows)),
        grid=(m // tm, n // tn),
        in_specs=in_specs,
        out_specs=[pl.BlockSpec((tm, tn), lambda i, j: (i, j)) for _ in out_dtypes],
        out_shape=[jax.ShapeDtypeStruct((m, n), dt) for dt in out_dtypes],
        compiler_params=_cparams("parallel", "arbitrary"),
        name=name,
    )(a, w, *rows)
    return outs


def _dtf_epilogue(acc, bias):
    x = acc + bias
    lane = lax.broadcasted_iota(jnp.int32, x.shape, 1)
    return (jnp.where(lane < SSD_HEADS, _softplus(x), -_softplus(-x)),)


def _cumsum_kernel(x_ref, o_ref):
    x = x_ref[...]
    n = x.shape[0]
    row = lax.broadcasted_iota(jnp.int32, x.shape, 0)
    s = 1
    while s < n:
        x = x + jnp.where(row >= s, pltpu.roll(x, s, 0), 0.0)
        s *= 2
    o_ref[...] = x


def seq_cumsum(x, *, seq):
    m, d = x.shape
    return pl.pallas_call(
        _cumsum_kernel,
        grid=(m // seq,),
        in_specs=[pl.BlockSpec((seq, d), lambda b: (b, 0))],
        out_specs=pl.BlockSpec((seq, d), lambda b: (b, 0)),
        out_shape=jax.ShapeDtypeStruct((m, d), F32),
        compiler_params=_cparams("parallel"),
        name="seq_cumsum",
    )(x)


def _pair_cols(mat, pair, lane_lo):
    a = mat[:, 2 * pair:2 * pair + 1]
    b = mat[:, 2 * pair + 1:2 * pair + 2]
    return jnp.where(lane_lo, a, b)


def _ssd_kernel(xbc_ref, dtf_ref, zs_ref, cw_ref, cb_ref, a_ref, dskip_ref, gn_ref,
                o_ref, st_out_ref, xbuf, xc, st, y_scr, xw_scr):
    c = pl.program_id(1)
    nc = pl.num_programs(1)
    cs = SSD_CHUNK
    halo = SUBLANES

    @pl.when(c == 0)
    def _():
        xbuf[0:halo, :] = jnp.zeros((halo, SSD_CONV_DIM), F32)
        st[...] = jnp.zeros_like(st)

    @pl.when(c > 0)
    def _():
        xbuf[0:halo, :] = xbuf[cs:cs + halo, :]

    xbuf[halo:halo + cs, :] = xbc_ref[...]

    cw = cw_ref[...]
    cb = cb_ref[...]
    col_chunk = 512
    for j in range(SSD_CONV_DIM // col_chunk):
        sl = slice(j * col_chunk, (j + 1) * col_chunk)
        acc = xbuf[halo - 3:halo - 3 + cs, sl] * cw[0:1, sl]
        for i in range(1, SSD_CONV):
            acc = acc + xbuf[halo - 3 + i:halo - 3 + i + cs, sl] * cw[i:i + 1, sl]
        xc[:, sl] = _silu(acc + cb[:, sl])

    dt = dtf_ref[...]
    da = dt * a_ref[...]
    r_i = lax.broadcasted_iota(jnp.int32, (cs, cs), 0)
    c_i = lax.broadcasted_iota(jnp.int32, (cs, cs), 1)
    causal = c_i <= r_i
    tril = jnp.where(causal, 1.0, 0.0)
    a_cs = jnp.dot(tril, da, preferred_element_type=F32, precision=lax.Precision.HIGHEST)
    a_cs_t = a_cs.T
    dt_t = dt.T
    last = a_cs[cs - 1:cs, :]
    ecs = jnp.exp(a_cs)
    wfac = dt * jnp.exp(last - a_cs)
    cdec = jnp.exp(last)

    lane_lo = lax.broadcasted_iota(jnp.int32, (cs, LANES), 1) < SSD_HEAD_DIM
    lane_lo_row = lane_lo[0:1, :]
    pairs_per_group = HEADS_PER_GROUP // 2
    for g in range(SSD_GROUPS):
        b_off = SSD_INNER + g * SSD_STATE
        c_off = SSD_INNER + SSD_GROUPS * SSD_STATE + g * SSD_STATE
        bg = xc[:, b_off:b_off + SSD_STATE]
        cg = xc[:, c_off:c_off + SSD_STATE].astype(BF16)
        cbm = _dot_nt(cg, bg.astype(BF16))
        yoff = _dot(cg, st[g].astype(BF16))
        cd_tiles = []
        for pr in range(pairs_per_group):
            pair = g * pairs_per_group + pr
            col0 = pair * LANES
            xs_pair = xc[:, col0:col0 + LANES]
            xs_bf = xs_pair.astype(BF16)
            yd = []
            for h in (2 * pair, 2 * pair + 1):
                seg = a_cs[:, h:h + 1] - a_cs_t[h:h + 1, :]
                decay = jnp.exp(jnp.where(causal, seg, -jnp.inf))
                mh = (cbm * decay * dt_t[h:h + 1, :]).astype(BF16)
                yd.append(_dot(mh, xs_bf))
            y_pair = jnp.where(lane_lo, yd[0], yd[1])
            y_pair = y_pair + _pair_cols(ecs, pair, lane_lo) * yoff[:, pr * LANES:(pr + 1) * LANES]
            y_pair = y_pair + xs_pair * dskip_ref[:, col0:col0 + LANES]
            y_scr[:, col0:col0 + LANES] = y_pair
            xw_scr[:, pr * LANES:(pr + 1) * LANES] = (xs_pair * _pair_cols(wfac, pair, lane_lo)).astype(BF16)
            cd_tiles.append(_pair_cols(cdec, pair, lane_lo_row))
        cd_row = jnp.concatenate(cd_tiles, axis=1)
        st[g] = st[g] * cd_row + _dot(bg.T.astype(BF16), xw_scr[...])

    v = y_scr[...] * zs_ref[...]
    ms = jnp.mean(v * v, axis=-1, keepdims=True)
    o_ref[...] = (v * lax.rsqrt(ms + EPS) * gn_ref[...]).astype(o_ref.dtype)

    @pl.when(c == nc - 1)
    def _():
        for g in range(SSD_GROUPS):
            st_out_ref[0, g * GROUP_WIDTH:(g + 1) * GROUP_WIDTH, :] = st[g].T


def ssd_prompt(xbc, dtf, zs, conv_w, conv_b, a_row, dskip_row, gnorm, *, batch, seq):
    nc = seq // SSD_CHUNK
    cs = SSD_CHUNK
    row = lambda b, c: (b * nc + c, 0)
    const = lambda b, c: (0, 0)
    return pl.pallas_call(
        _ssd_kernel,
        grid=(batch, nc),
        in_specs=[
            pl.BlockSpec((cs, SSD_CONV_DIM), row),
            pl.BlockSpec((cs, DTF_WIDTH), row),
            pl.BlockSpec((cs, SSD_INNER), row),
            pl.BlockSpec((SSD_CONV, SSD_CONV_DIM), const),
            pl.BlockSpec((1, SSD_CONV_DIM), const),
            pl.BlockSpec((1, DTF_WIDTH), const),
            pl.BlockSpec((1, SSD_INNER), const),
            pl.BlockSpec((1, SSD_INNER), const),
        ],
        out_specs=[
            pl.BlockSpec((cs, SSD_INNER), row),
            pl.BlockSpec((1, SSD_HEADS * SSD_HEAD_DIM, SSD_STATE), lambda b, c: (b, 0, 0)),
        ],
        out_shape=[
            jax.ShapeDtypeStruct((batch * seq, SSD_INNER), BF16),
            jax.ShapeDtypeStruct((batch, SSD_HEADS * SSD_HEAD_DIM, SSD_STATE), F32),
        ],
        scratch_shapes=[
            pltpu.VMEM((cs + SUBLANES, SSD_CONV_DIM), F32),
            pltpu.VMEM((cs, SSD_CONV_DIM), F32),
            pltpu.VMEM((SSD_GROUPS, SSD_STATE, GROUP_WIDTH), F32),
            pltpu.VMEM((cs, SSD_INNER), F32),
            pltpu.VMEM((cs, GROUP_WIDTH), BF16),
        ],
        compiler_params=_cparams("parallel", "arbitrary"),
        name="ssd_prompt",
    )(xbc, dtf, zs, conv_w, conv_b, a_row, dskip_row, gnorm)


def _fox_kernel(q_ref, k_ref, v_ref, cq_ref, ck_ref, o_ref, *, seq, row_blocks, heads, scale):
    dh = FOX_HEAD_DIM
    blk = seq // row_blocks
    for h in range(heads):
        cols = slice(h * dh, (h + 1) * dh)
        for r in range(row_blocks):
            r0, kw = r * blk, (r + 1) * blk
            s = _dot_nt(q_ref[r0:r0 + blk, cols], k_ref[0:kw, cols]) * scale
            s = s + cq_ref[0, h, r0:r0 + blk, :] - ck_ref[0, h, :, 0:kw]
            r_i = lax.broadcasted_iota(jnp.int32, (blk, kw), 0) + r0
            c_i = lax.broadcasted_iota(jnp.int32, (blk, kw), 1)
            s = jnp.where(c_i <= r_i, s, -jnp.inf)
            p = jnp.exp(s - jnp.max(s, axis=-1, keepdims=True))
            l = jnp.sum(p, axis=-1, keepdims=True)
            o_ref[r0:r0 + blk, cols] = (_dot(p.astype(BF16), v_ref[0:kw, cols]) / l).astype(o_ref.dtype)


def fox_prompt(q, k, v, cq, ck, *, batch, seq, row_blocks, heads):
    width = heads * FOX_HEAD_DIM
    tokens = pl.BlockSpec((seq, width), lambda b, h: (b, h))
    return pl.pallas_call(
        functools.partial(_fox_kernel, seq=seq, row_blocks=row_blocks, heads=heads, scale=FOX_HEAD_DIM ** -0.5),
        grid=(batch, FOX_HEADS // heads),
        in_specs=[tokens, tokens, tokens,
                  pl.BlockSpec((1, heads, seq, 1), lambda b, h: (b, h, 0, 0)),
                  pl.BlockSpec((1, heads, 1, seq), lambda b, h: (b, h, 0, 0))],
        out_specs=tokens,
        out_shape=jax.ShapeDtypeStruct((batch * seq, FOX_WIDTH), BF16),
        compiler_params=_cparams("parallel", "arbitrary"),
        name="fox_prompt",
    )(q, k, v, cq, ck)


def _mem_kernel(q_ref, k_ref, v_ref, o_ref, *, scale):
    dh = MEM_HEAD_DIM
    for h in range(MEM_HEADS):
        cols = slice(h * dh, (h + 1) * dh)
        s = _dot_nt(q_ref[:, cols], k_ref[:, cols]) * scale
        e = jnp.exp(s - jnp.max(s, axis=-1, keepdims=True))
        p = e / jnp.sum(e, axis=-1, keepdims=True)
        o_ref[:, cols] = _dot(p.astype(BF16), v_ref[:, cols]).astype(o_ref.dtype)


def mem_prompt_attend(q, k, v, *, batch, seq, n_mem, tq):
    nq = seq // tq
    return pl.pallas_call(
        functools.partial(_mem_kernel, scale=MEM_HEAD_DIM ** -0.5),
        grid=(batch, nq),
        in_specs=[
            pl.BlockSpec((tq, MEM_WIDTH), lambda b, i: (b * nq + i, 0)),
            pl.BlockSpec((n_mem, MEM_WIDTH), lambda b, i: (b, 0)),
            pl.BlockSpec((n_mem, MEM_WIDTH), lambda b, i: (b, 0)),
        ],
        out_specs=pl.BlockSpec((tq, MEM_WIDTH), lambda b, i: (b * nq + i, 0)),
        out_shape=jax.ShapeDtypeStruct((batch * seq, MEM_WIDTH), BF16),
        compiler_params=_cparams("parallel", "arbitrary"),
        name="mem_prompt_attend",
    )(q, k, v)


def _merge_kernel(os_ref, of_ref, om_ref, ws_ref, wf_ref, wm_ref, g0_ref, g1_ref, g2_ref, o_ref):
    u = g0_ref[...] * _dot(os_ref[...], ws_ref[...].astype(BF16))
    u = u + g1_ref[...] * _dot(of_ref[...], wf_ref[...].astype(BF16))
    u = u + g2_ref[...] * _dot(om_ref[...], wm_ref[...].astype(BF16))
    o_ref[...] = u.astype(o_ref.dtype)


def branch_merge(o_ssd, o_fox, o_mem, w_branch, gates, *, tm, tn):
    m = o_ssd.shape[0]
    tm = min(tm, m)
    nj = D_MODEL // tn
    act = lambda k: pl.BlockSpec((tm, k), lambda i, j: (i, 0))
    gate = lambda br: pl.BlockSpec((tm, tn), lambda i, j: (i, br * nj + j))

    def wgt(row0, k):
        assert row0 % k == 0
        return pl.BlockSpec((None, k, tn), lambda i, j: (0, row0 // k, j))

    return pl.pallas_call(
        _merge_kernel,
        grid=(m // tm, nj),
        in_specs=[act(SSD_INNER), act(FOX_WIDTH), act(MEM_WIDTH),
                  wgt(0, SSD_INNER), wgt(SSD_INNER, FOX_WIDTH), wgt(SSD_INNER + FOX_WIDTH, MEM_WIDTH),
                  gate(0), gate(1), gate(2)],
        out_specs=pl.BlockSpec((tm, tn), lambda i, j: (i, j)),
        out_shape=jax.ShapeDtypeStruct((m, D_MODEL), BF16),
        compiler_params=_cparams("parallel", "arbitrary"),
        name="branch_merge",
    )(o_ssd, o_fox, o_mem, w_branch, w_branch, w_branch, gates, gates, gates)


def _outproj_kernel(a_ref, w_ref, x_ref, g_ref, h_ref, hn_ref):
    h = x_ref[...] + _dot(a_ref[...], w_ref[...])
    h_ref[...] = h
    ms = jnp.mean(h * h, axis=-1, keepdims=True)
    hn_ref[...] = (h * lax.rsqrt(ms + EPS) * g_ref[...]).astype(hn_ref.dtype)


def out_proj(merged, w_out, x, g_ffn, *, tm):
    m = x.shape[0]
    tm = min(tm, m)
    row = lambda i: (i, 0)
    const = lambda i: (0, 0)
    return pl.pallas_call(
        _outproj_kernel,
        grid=(m // tm,),
        in_specs=[pl.BlockSpec((tm, D_MODEL), row), pl.BlockSpec((D_MODEL, D_MODEL), const),
                  pl.BlockSpec((tm, D_MODEL), row), pl.BlockSpec((1, D_MODEL), const)],
        out_specs=[pl.BlockSpec((tm, D_MODEL), row), pl.BlockSpec((tm, D_MODEL), row)],
        out_shape=[jax.ShapeDtypeStruct((m, D_MODEL), F32), jax.ShapeDtypeStruct((m, D_MODEL), BF16)],
        compiler_params=_cparams("parallel"),
        name="out_proj",
    )(merged, w_out, x, g_ffn.reshape(1, D_MODEL))


def _ffn_kernel(hn_ref, wg_ref, wu_ref, cw_ref, cb_ref, act_ref, buf_ref, gbuf, *, tiles_per_seq):
    i = pl.program_id(1)
    tm = hn_ref.shape[0]
    halo = SUBLANES

    @pl.when(i % tiles_per_seq == 0)
    def _():
        gbuf[0:halo, :] = jnp.zeros((halo, gbuf.shape[1]), F32)

    @pl.when(i % tiles_per_seq != 0)
    def _():
        gbuf[0:halo, :] = gbuf[tm:tm + halo, :]

    hn = hn_ref[...]
    gbuf[halo:halo + tm, :] = _dot(hn, wg_ref[...].astype(BF16))
    cw = cw_ref[...]
    conv = gbuf[halo - 2:halo - 2 + tm, :] * cw[0:1, :]
    conv = conv + gbuf[halo - 1:halo - 1 + tm, :] * cw[1:2, :]
    conv = conv + gbuf[halo:halo + tm, :] * cw[2:3, :]
    conv = conv + cb_ref[...]
    act_ref[...] = (_silu(conv) * _dot(hn, wu_ref[...].astype(BF16))).astype(act_ref.dtype)
    buf_ref[0] = gbuf[halo + tm - 2:halo + tm, :]


def ffn_gate_up(hn, w_gate, w_up, conv_w, conv_b, *, batch, seq, tm, tn):
    m = hn.shape[0]
    tiles_per_seq = seq // tm
    return pl.pallas_call(
        functools.partial(_ffn_kernel, tiles_per_seq=tiles_per_seq),
        grid=(D_FF // tn, m // tm),
        in_specs=[
            pl.BlockSpec((tm, D_MODEL), lambda j, i: (i, 0)),
            pl.BlockSpec((None, D_MODEL, tn), lambda j, i: (0, 0, j)),
            pl.BlockSpec((None, D_MODEL, tn), lambda j, i: (0, 0, j)),
            pl.BlockSpec((FFN_CONV, tn), lambda j, i: (0, j)),
            pl.BlockSpec((1, tn), lambda j, i: (0, j)),
        ],
        out_specs=[
            pl.BlockSpec((tm, tn), lambda j, i: (i, j)),
            pl.BlockSpec((1, FFN_CONV - 1, tn), lambda j, i: (i // tiles_per_seq, 0, j)),
        ],
        out_shape=[
            jax.ShapeDtypeStruct((m, D_FF), BF16),
            jax.ShapeDtypeStruct((batch, FFN_CONV - 1, D_FF), F32),
        ],
        scratch_shapes=[pltpu.VMEM((tm + SUBLANES, tn), F32)],
        compiler_params=_cparams("parallel", "arbitrary"),
        name="ffn_gate_up",
    )(hn, w_gate, w_up, conv_w, conv_b)


def _down_kernel(a_ref, w_ref, h_ref, g_ref, y_ref, acc):
    k = pl.program_id(1)

    @pl.when(k == 0)
    def _():
        acc[...] = h_ref[...]

    acc[...] += _dot(a_ref[...], w_ref[...])

    @pl.when(k == pl.num_programs(1) - 1)
    def _():
        h = acc[...]
        ms = jnp.mean(h * h, axis=-1, keepdims=True)
        y_ref[...] = h * lax.rsqrt(ms + EPS) * g_ref[...]


def ffn_down(act, w_down, h, g_final, *, tm, tk):
    m = h.shape[0]
    tm = min(tm, m)
    return pl.pallas_call(
        _down_kernel,
        grid=(m // tm, D_FF // tk),
        in_specs=[pl.BlockSpec((tm, tk), lambda i, k: (i, k)), pl.BlockSpec((tk, D_MODEL), lambda i, k: (k, 0)),
                  pl.BlockSpec((tm, D_MODEL), lambda i, k: (i, 0)), pl.BlockSpec((1, D_MODEL), lambda i, k: (0, 0))],
        out_specs=pl.BlockSpec((tm, D_MODEL), lambda i, k: (i, 0)),
        out_shape=jax.ShapeDtypeStruct((m, D_MODEL), F32),
        scratch_shapes=[pltpu.VMEM((tm, D_MODEL), F32)],
        compiler_params=_cparams("parallel", "arbitrary"),
        name="ffn_down",
    )(act, w_down, h, g_final.reshape(1, D_MODEL))


def _ssd_s_prep_kernel(xbc_ref, buf_ref, cw_ref, cb_ref, dtf_ref, a_ref, expand_ref, xc_ref, cols_ref):
    cw = cw_ref[...]
    acc = buf_ref[0] * cw[0:1, :]
    acc = acc + buf_ref[1] * cw[1:2, :]
    acc = acc + buf_ref[2] * cw[2:3, :]
    acc = acc + xbc_ref[...] * cw[3:4, :]
    xcv = _silu(acc + cb_ref[...])
    xc_ref[...] = xcv
    dt = dtf_ref[...]
    decay = jnp.exp(dt * a_ref[...])
    nb = dt.shape[0]
    parts = _split3(dt) + _split3(decay)
    stacked = jnp.concatenate(parts, axis=0)
    ex = _dot(stacked, expand_ref[...])
    dt_exp = ex[0:nb] + ex[nb:2 * nb] + ex[2 * nb:3 * nb]
    dec_exp = ex[3 * nb:4 * nb] + ex[4 * nb:5 * nb] + ex[5 * nb:6 * nb]
    xdt = xcv[:, 0:SSD_INNER] * dt_exp
    pad = jnp.zeros((LANES - 2 * nb, SSD_INNER), F32)
    cols_ref[...] = jnp.concatenate([xdt, dec_exp, pad], axis=0).T


def ssd_sample_prep(xbc, buf_t, conv_w, conv_b, dtf, a_row, expand):
    nb = xbc.shape[0]
    full = lambda s: pl.BlockSpec(s, lambda i: (0,) * len(s))
    return pl.pallas_call(
        _ssd_s_prep_kernel,
        grid=(1,),
        in_specs=[full(xbc.shape), full(buf_t.shape), full(conv_w.shape), full(conv_b.shape), full(dtf.shape),
                  full(a_row.shape), full(expand.shape)],
        out_specs=[full((nb, SSD_CONV_DIM)), full((SSD_INNER, LANES))],
        out_shape=[jax.ShapeDtypeStruct((nb, SSD_CONV_DIM), F32), jax.ShapeDtypeStruct((SSD_INNER, LANES), F32)],
        compiler_params=_cparams("arbitrary"),
        name="ssd_sample_prep",
    )(xbc, buf_t, conv_w, conv_b, dtf, a_row, expand)


def _ssd_s_step_kernel(cols_ref, h0_ref, xc_ref, zs_ref, dskip_ref, gn_ref, hn_ref, o_ref, *, nb):
    b = pl.program_id(0)
    cols = cols_ref[...]
    hi, mid, lo = _split3(cols)
    lhs = jnp.concatenate([hi, mid, lo], axis=1)
    r_i = lax.broadcasted_iota(jnp.int32, (3 * LANES, 2 * LANES), 0) % LANES
    c_i = lax.broadcasted_iota(jnp.int32, (3 * LANES, 2 * LANES), 1)
    sel = jnp.where(r_i == jnp.where(c_i < LANES, b, nb + b), 1.0, 0.0).astype(BF16)
    picked = _dot(lhs, sel)
    xb = picked[:, 0:LANES]
    db = picked[:, LANES:2 * LANES]
    xc = xc_ref[0]
    y_parts = []
    for g in range(SSD_GROUPS):
        rows = slice(g * GROUP_WIDTH, (g + 1) * GROUP_WIDTH)
        b_row = xc[:, SSD_INNER + g * SSD_STATE:SSD_INNER + (g + 1) * SSD_STATE]
        c_off = SSD_INNER + SSD_GROUPS * SSD_STATE + g * SSD_STATE
        c_row = xc[:, c_off:c_off + SSD_STATE]
        hn = h0_ref[0, rows, :] * db[rows, :] + xb[rows, :] * b_row
        hn_ref[0, rows, :] = hn
        c8 = jnp.broadcast_to(c_row, (SUBLANES, SSD_STATE)).astype(BF16)
        y_parts.append(_dot_nt(c8, hn.astype(BF16))[0:1, :])
    xs = xc[:, 0:SSD_INNER]
    y = jnp.concatenate(y_parts, axis=1) + xs * dskip_ref[...]
    v = y * zs_ref[0]
    ms = jnp.mean(v * v, axis=-1, keepdims=True)
    o_ref[0] = (v * lax.rsqrt(ms + EPS) * gn_ref[...]).astype(o_ref.dtype)


def ssd_sample_step(cols, h0, xc, zs, dskip_row, gnorm):
    nb = h0.shape[0]
    rows = SSD_HEADS * SSD_HEAD_DIM
    const2 = lambda b: (0, 0)
    per_b = lambda b: (b, 0, 0)
    return pl.pallas_call(
        functools.partial(_ssd_s_step_kernel, nb=nb),
        grid=(nb,),
        in_specs=[
            pl.BlockSpec((SSD_INNER, LANES), const2),
            pl.BlockSpec((1, rows, SSD_STATE), per_b),
            pl.BlockSpec((1, 1, SSD_CONV_DIM), per_b),
            pl.BlockSpec((1, 1, SSD_INNER), per_b),
            pl.BlockSpec((1, SSD_INNER), const2),
            pl.BlockSpec((1, SSD_INNER), const2),
        ],
        out_specs=[pl.BlockSpec((1, rows, SSD_STATE), per_b), pl.BlockSpec((1, 1, SSD_INNER), per_b)],
        out_shape=[jax.ShapeDtypeStruct((nb, rows, SSD_STATE), F32), jax.ShapeDtypeStruct((nb, 1, SSD_INNER), BF16)],
        compiler_params=_cparams("parallel"),
        name="ssd_sample_step",
    )(cols, h0, xc.reshape(nb, 1, SSD_CONV_DIM), zs.reshape(nb, 1, SSD_INNER), dskip_row, gnorm)


def _head_matched(shape, n_heads):
    r_i = lax.broadcasted_iota(jnp.int32, shape, 0)
    c_i = lax.broadcasted_iota(jnp.int32, shape, 1)
    return c_i % n_heads == r_i % n_heads


def _fox_s_kernel(pt_ref, *refs, n_steps, scale):
    del pt_ref
    pp = PAGES_PER_STEP
    nh = FOX_HEADS
    q_ref, kn_ref, vn_ref, lfn_ref, suf_ref = refs[0:5]
    k_refs = refs[5:5 + pp]
    v_refs = refs[5 + pp:5 + 2 * pp]
    lf_refs = refs[5 + 2 * pp:5 + 3 * pp]
    o_ref = refs[5 + 3 * pp]
    m_scr, l_scr, acc_scr, carry = refs[6 + 3 * pp:]
    j = pl.program_id(1)

    @pl.when(j == 0)
    def _():
        m_scr[...] = jnp.full_like(m_scr, -jnp.inf)
        l_scr[...] = jnp.zeros_like(l_scr)
        acc_scr[...] = jnp.zeros_like(acc_scr)
        carry[...] = lfn_ref[0]

    q = q_ref[0].astype(BF16)
    page = k_refs[0].shape[2]
    rows = page * nh // LANES
    keep = _head_matched((nh, page * nh), nh)
    row_i = lax.broadcasted_iota(jnp.int32, (rows, LANES), 0)

    run = carry[...]
    scores = []
    for i in range(pp):
        lfd = lf_refs[i][0]
        r = _dot(jnp.concatenate(_split3(lfd), axis=0), suf_ref[...])
        within = r[0:rows, 0:LANES] + r[rows:2 * rows, 0:LANES] + r[2 * rows:3 * rows, 0:LANES]
        rowtot = r[0:rows, LANES:] + r[rows:2 * rows, LANES:] + r[2 * rows:3 * rows, LANES:]
        incl = rowtot
        sft = 1
        while sft < rows:
            incl = incl + jnp.where(row_i + sft < rows, pltpu.roll(incl, rows - sft, 0), 0.0)
            sft *= 2
        bias_d = within + (incl - rowtot) + run
        run = run + incl[0:1, :]
        bias = jnp.concatenate([jnp.broadcast_to(bias_d[t:t + 1, :], (nh, LANES)) for t in range(rows)], axis=1)
        k2 = k_refs[i][0, 0].reshape(page * nh, FOX_HEAD_DIM).astype(BF16)
        scores.append(jnp.where(keep, _dot_nt(q, k2) * scale + bias, -jnp.inf))
    carry[...] = run

    m_old = m_scr[...]
    m_new = m_old
    for s in scores:
        m_new = jnp.maximum(m_new, jnp.max(s, axis=-1, keepdims=True))
    alpha = jnp.exp(m_old - m_new)
    l = alpha * l_scr[...]
    acc = alpha * acc_scr[...]
    for i, s in enumerate(scores):
        p = jnp.exp(s - m_new)
        l = l + jnp.sum(p, axis=-1, keepdims=True)
        v2 = v_refs[i][0, 0].reshape(page * nh, FOX_HEAD_DIM).astype(BF16)
        acc = acc + _dot(p.astype(BF16), v2)
    m_scr[...] = m_new
    l_scr[...] = l
    acc_scr[...] = acc

    @pl.when(j == n_steps - 1)
    def _():
        s_self = jnp.sum(q_ref[0] * kn_ref[0], axis=-1, keepdims=True) * scale
        m_old = m_scr[...]
        m_new = jnp.maximum(m_old, s_self)
        alpha = jnp.exp(m_old - m_new)
        p_self = jnp.exp(s_self - m_new)
        l = alpha * l_scr[...] + p_self
        acc = alpha * acc_scr[...] + p_self * vn_ref[0]
        o_ref[0] = (acc / l).astype(o_ref.dtype)


def fox_sample(page_table, q, k_new, v_new, lf_new_tiled, k_pool, v_pool, lf_pool_dense):
    nb, n_pages = page_table.shape
    page = k_pool.shape[2]
    rows = lf_pool_dense.shape[1]
    pp = PAGES_PER_STEP
    n_steps = n_pages // pp
    r_i = lax.broadcasted_iota(jnp.int32, (LANES, 2 * LANES), 0)
    c_i = lax.broadcasted_iota(jnp.int32, (LANES, 2 * LANES), 1)
    same = r_i % FOX_HEADS == c_i % FOX_HEADS
    suf = (same & ((c_i >= LANES) | (r_i > c_i))).astype(BF16)

    def newest_first(i):
        return lambda b, j, pt: pt[b, n_pages - 1 - (j * pp + i)]

    def kv_spec(i):
        pick = newest_first(i)
        return pl.BlockSpec((1, 1, page, FOX_HEADS, FOX_HEAD_DIM), lambda b, j, pt: (0, pick(b, j, pt), 0, 0, 0))

    def lf_spec(i):
        pick = newest_first(i)
        return pl.BlockSpec((1, rows, LANES), lambda b, j, pt: (pick(b, j, pt), 0, 0))

    per_b = lambda shape: pl.BlockSpec((1,) + shape, lambda b, j, pt: (b, 0, 0))
    hd = (FOX_HEADS, FOX_HEAD_DIM)
    in_specs = [per_b(hd), per_b(hd), per_b(hd), per_b((1, LANES)),
                pl.BlockSpec((LANES, 2 * LANES), lambda b, j, pt: (0, 0))]
    in_specs += [kv_spec(i) for i in range(pp)]
    in_specs += [kv_spec(i) for i in range(pp)]
    in_specs += [lf_spec(i) for i in range(pp)]
    grid_spec = pltpu.PrefetchScalarGridSpec(
        num_scalar_prefetch=1,
        grid=(nb, n_steps),
        in_specs=in_specs,
        out_specs=per_b(hd),
        scratch_shapes=[
            pltpu.VMEM((FOX_HEADS, 1), F32),
            pltpu.VMEM((FOX_HEADS, 1), F32),
            pltpu.VMEM(hd, F32),
            pltpu.VMEM((1, LANES), F32),
        ],
    )
    return pl.pallas_call(
        functools.partial(_fox_s_kernel, n_steps=n_steps, scale=FOX_HEAD_DIM ** -0.5),
        grid_spec=grid_spec,
        out_shape=jax.ShapeDtypeStruct((nb,) + hd, BF16),
        compiler_params=_cparams("parallel", "arbitrary"),
        name="fox_sample",
    )(page_table, q.reshape((nb,) + hd), k_new.reshape((nb,) + hd), v_new.reshape((nb,) + hd),
      lf_new_tiled.reshape(nb, 1, LANES), suf,
      *([k_pool] * pp), *([v_pool] * pp), *([lf_pool_dense] * pp))


def _mem_s_kernel(q_ref, k_ref, v_ref, o_ref, *, scale):
    n_mem = k_ref.shape[2]
    k2 = k_ref[0, 0].reshape(n_mem * MEM_HEADS, MEM_HEAD_DIM).astype(BF16)
    v2 = v_ref[0, 0].reshape(n_mem * MEM_HEADS, MEM_HEAD_DIM).astype(BF16)
    q = q_ref[0]
    keep = _head_matched((q.shape[0], n_mem * MEM_HEADS), MEM_HEADS)
    s = jnp.where(keep, _dot_nt(q, k2) * scale, -jnp.inf)
    e = jnp.exp(s - jnp.max(s, axis=-1, keepdims=True))
    p = e / jnp.sum(e, axis=-1, keepdims=True)
    o_ref[0] = _dot(p.astype(BF16), v2).astype(o_ref.dtype)


def mem_sample_attend(q_rep, k, v):
    _, nb, n_mem, _, _ = k.shape
    rep = q_rep.shape[1]
    kv = pl.BlockSpec((1, 1, n_mem, MEM_HEADS, MEM_HEAD_DIM), lambda b: (0, b, 0, 0, 0))
    return pl.pallas_call(
        functools.partial(_mem_s_kernel, scale=MEM_HEAD_DIM ** -0.5),
        grid=(nb,),
        in_specs=[pl.BlockSpec((1, rep, MEM_HEAD_DIM), lambda b: (b, 0, 0)), kv, kv],
        out_specs=pl.BlockSpec((1, rep, MEM_HEAD_DIM), lambda b: (b, 0, 0)),
        out_shape=jax.ShapeDtypeStruct((nb, rep, MEM_HEAD_DIM), BF16),
        compiler_params=_cparams("parallel"),
        name="mem_sample_attend",
    )(q_rep, k, v)


def _ffn_s_kernel(g_ref, u_ref, buf_ref, cw_ref, cb_ref, o_ref):
    cw = cw_ref[...]
    conv = buf_ref[0] * cw[0:1, :]
    conv = conv + buf_ref[1] * cw[1:2, :]
    conv = conv + g_ref[...] * cw[2:3, :]
    conv = conv + cb_ref[...]
    o_ref[...] = (_silu(conv) * u_ref[...]).astype(o_ref.dtype)


def ffn_sample_act(g, u, buf_t, conv_w, conv_b):
    full = lambda s: pl.BlockSpec(s, lambda i: (0,) * len(s))
    return pl.pallas_call(
        _ffn_s_kernel,
        grid=(1,),
        in_specs=[full(g.shape), full(u.shape), full(buf_t.shape), full(conv_w.shape), full(conv_b.shape)],
        out_specs=full(g.shape),
        out_shape=jax.ShapeDtypeStruct(g.shape, BF16),
        compiler_params=_cparams("arbitrary"),
        name="ffn_sample_act",
    )(g, u, buf_t, conv_w, conv_b)


def _project(xn, w, *, tm, tn):
    mm = functools.partial(matmul, xn, tm=tm)
    (zs,) = mm(w["z"], tn=tn, out_dtypes=(F32,), epilogue=lambda acc: (_silu(acc),), name="proj_z")
    (xbc,) = mm(w["xbc"], tn=tn, out_dtypes=(F32,), name="proj_xbc")
    (dtf,) = mm(w["dtf"], tn=DTF_WIDTH, out_dtypes=(F32,), epilogue=_dtf_epilogue, rows=(w["dtf_bias"],),
                name="proj_dtf")
    (q,) = mm(w["q"], tn=tn, out_dtypes=(BF16,), name="proj_q")
    k, k_bf = mm(w["k"], tn=tn, out_dtypes=(F32, BF16), epilogue=lambda acc: (acc, acc), name="proj_k")
    v, v_bf = mm(w["v"], tn=tn, out_dtypes=(F32, BF16), epilogue=lambda acc: (acc, acc), name="proj_v")
    (mq,) = mm(w["mq"], tn=tn, out_dtypes=(BF16,), name="proj_mq")
    (gates,) = mm(w["gates"], tn=tn, out_dtypes=(F32,), epilogue=lambda acc: (jax.nn.sigmoid(acc),),
                  name="proj_gates")
    return dict(zs=zs, xbc=xbc, dtf=dtf, q=q, k=k, k_bf=k_bf, v=v, v_bf=v_bf, mq=mq, gates=gates)


def kernel(x_prompt, x_sample, cache_fox_k, cache_fox_v, cache_fox_logf, cache_mem_k, cache_mem_v, state_ssd,
           state_ssd_conv, state_ffn_conv, page_table, mem_prompt, g_mix, w_in, ssd_conv_w, ssd_conv_b,
           ssd_dt_bias, ssd_a_log, ssd_d, ssd_g_norm, fox_b_forget, g_mem, w_mem_kv, w_branch, w_out, g_ffn,
           w_ffn_gate, w_ffn_up, ffn_conv_w, ffn_conv_b, w_ffn_down, g_final):
    depth = w_in.shape[0]
    assert depth == 1, "single-layer trunk"
    bp, seq, d = x_prompt.shape
    nb, dec_seq, _ = x_sample.shape
    assert d == D_MODEL and dec_seq == 1 and seq % SSD_CHUNK == 0
    n_mem = mem_prompt.shape[1]
    n_pool, page = cache_fox_k.shape[1], cache_fox_k.shape[2]
    assert page_table.shape[1] % PAGES_PER_STEP == 0
    mp = bp * seq

    wi = w_in[0]
    offs = [0]
    for width in (SSD_INNER, SSD_CONV_DIM, SSD_HEADS, FOX_WIDTH, FOX_WIDTH, FOX_WIDTH, FOX_HEADS, MEM_WIDTH,
                  N_BRANCH * D_MODEL):
        offs.append(offs[-1] + width)
    assert offs[-1] == wi.shape[1]
    seg = lambda i: wi[:, offs[i]:offs[i + 1]].astype(BF16)
    dtf_pad = DTF_WIDTH - SSD_HEADS - FOX_HEADS
    w = dict(
        z=seg(0), xbc=seg(1), q=seg(3), k=seg(4), v=seg(5), mq=seg(7), gates=seg(8),
        dtf=jnp.concatenate([seg(2), seg(6), jnp.zeros((D_MODEL, dtf_pad), BF16)], axis=1),
        dtf_bias=jnp.concatenate([ssd_dt_bias[0], fox_b_forget[0], jnp.zeros((dtf_pad,), F32)]).reshape(1, DTF_WIDTH),
    )
    w_o = w_out[0].astype(BF16)
    w_d = w_ffn_down[0].astype(BF16)
    w_mk = w_mem_kv[0][:, :MEM_WIDTH].astype(BF16)
    w_mv = w_mem_kv[0][:, MEM_WIDTH:].astype(BF16)
    a_row = jnp.concatenate([-jnp.exp(ssd_a_log[0]), jnp.zeros((DTF_WIDTH - SSD_HEADS,), F32)]).reshape(1, DTF_WIDTH)
    dskip_row = jnp.repeat(ssd_d[0], SSD_HEAD_DIM).reshape(1, SSD_INNER)
    gnorm_row = ssd_g_norm[0].reshape(1, SSD_INNER)
    conv_w = ssd_conv_w[0]
    conv_b = ssd_conv_b[0].reshape(1, SSD_CONV_DIM)
    fconv_w = ffn_conv_w[0]
    fconv_b = ffn_conv_b[0].reshape(1, D_FF)

    xp = x_prompt.reshape(mp, D_MODEL)
    pr = _project(rms_norm(xp, g_mix[0], tm=512), w, tm=2048, tn=512)

    mem_n = rms_norm(mem_prompt.reshape(bp * n_mem, D_MODEL), g_mem[0], tm=512)
    mk_p, mk_bf = matmul(mem_n, w_mk, tm=1024, tn=512, out_dtypes=(F32, BF16), epilogue=lambda acc: (acc, acc),
                         name="proj_mem_k")
    mv_p, mv_bf = matmul(mem_n, w_mv, tm=1024, tn=512, out_dtypes=(F32, BF16), epilogue=lambda acc: (acc, acc),
                         name="proj_mem_v")

    o_ssd_p, st_p = ssd_prompt(pr["xbc"], pr["dtf"], pr["zs"], conv_w, conv_b, a_row, dskip_row, gnorm_row,
                               batch=bp, seq=seq)

    c_all = seq_cumsum(pr["dtf"], seq=seq)
    c_p = c_all[:, SSD_HEADS:SSD_HEADS + FOX_HEADS].reshape(bp, seq, FOX_HEADS).transpose(0, 2, 1)
    o_fox_p = fox_prompt(pr["q"], pr["k_bf"], pr["v_bf"], c_p.reshape(bp, FOX_HEADS, seq, 1),
                         c_p.reshape(bp, FOX_HEADS, 1, seq), batch=bp, seq=seq, row_blocks=4, heads=4)
    o_mem_p = mem_prompt_attend(pr["mq"], mk_bf, mv_bf, batch=bp, seq=seq, n_mem=n_mem, tq=1024)

    merged_p = branch_merge(o_ssd_p, o_fox_p, o_mem_p, w_branch, pr["gates"], tm=1024, tn=256)
    h_p, hn_p = out_proj(merged_p, w_o, xp, g_ffn[0], tm=512)
    act_p, fbuf_p = ffn_gate_up(hn_p, w_ffn_gate, w_ffn_up, fconv_w, fconv_b, batch=bp, seq=seq, tm=1024, tn=512)
    y_p = ffn_down(act_p, w_d, h_p, g_final, tm=512, tk=D_FF // 2)

    xs_ = x_sample.reshape(nb, D_MODEL)
    sm = _project(rms_norm(xs_, g_mix[0], tm=nb), w, tm=nb, tn=2048)

    buf_t = state_ssd_conv[0].transpose(1, 0, 2)
    h_i = lax.broadcasted_iota(jnp.int32, (DTF_WIDTH, SSD_INNER), 0)
    col_i = lax.broadcasted_iota(jnp.int32, (DTF_WIDTH, SSD_INNER), 1)
    expand = (col_i // SSD_HEAD_DIM == h_i).astype(BF16)
    xc_s, cols_s = ssd_sample_prep(sm["xbc"], buf_t, conv_w, conv_b, sm["dtf"], a_row, expand)
    st_s, o_ssd_s = ssd_sample_step(cols_s, state_ssd[0].reshape(nb, SSD_HEADS * SSD_HEAD_DIM, SSD_STATE), xc_s,
                                    sm["zs"], dskip_row, gnorm_row)

    lf_s = sm["dtf"][:, SSD_HEADS:SSD_HEADS + FOX_HEADS]
    lf_dense = cache_fox_logf[0].reshape(n_pool, page * FOX_HEADS // LANES, LANES)
    o_fox_s = fox_sample(page_table, sm["q"].astype(F32), sm["k"], sm["v"], jnp.tile(lf_s, (1, LANES // FOX_HEADS)),
                         cache_fox_k, cache_fox_v, lf_dense)
    mq_rep = jnp.tile(sm["mq"].reshape(nb, MEM_HEADS, MEM_HEAD_DIM), (1, 2 * SUBLANES // MEM_HEADS, 1))
    o_mem_s = mem_sample_attend(mq_rep, cache_mem_k, cache_mem_v)
    o_mem_s = o_mem_s[:, :MEM_HEADS].reshape(nb, MEM_WIDTH)

    merged_s = branch_merge(o_ssd_s.reshape(nb, SSD_INNER), o_fox_s.reshape(nb, FOX_WIDTH), o_mem_s, w_branch,
                            sm["gates"], tm=nb, tn=512)
    h_s, hn_s = out_proj(merged_s, w_o, xs_, g_ffn[0], tm=nb)
    (g_s,) = matmul(hn_s, w_ffn_gate, cols=(0, D_FF), tm=nb, tn=D_FF // 4, out_dtypes=(F32,), name="ffn_gate_s")
    (u_s,) = matmul(hn_s, w_ffn_up, cols=(0, D_FF), tm=nb, tn=D_FF // 4, out_dtypes=(F32,), name="ffn_up_s")
    act_s = ffn_sample_act(g_s, u_s, state_ffn_conv[0].transpose(1, 0, 2), fconv_w, fconv_b)
    y_s = ffn_down(act_s, w_d, h_s, g_final, tm=nb, tk=D_FF // 2)

    lf_p = pr["dtf"][:, SSD_HEADS:SSD_HEADS + FOX_HEADS]
    heads5 = lambda t, b, l: t.reshape(1, b, l, FOX_HEADS, FOX_HEAD_DIM)
    return (
        y_p.reshape(bp, seq, D_MODEL),
        y_s.reshape(nb, 1, D_MODEL),
        heads5(pr["k"], bp, seq),
        heads5(pr["v"], bp, seq),
        lf_p.reshape(1, bp, seq, FOX_HEADS),
        mk_p.reshape(1, bp, n_mem, MEM_HEADS, MEM_HEAD_DIM),
        mv_p.reshape(1, bp, n_mem, MEM_HEADS, MEM_HEAD_DIM),
        st_p.reshape(1, bp, SSD_HEADS, SSD_HEAD_DIM, SSD_STATE),
        pr["xbc"].reshape(bp, seq, SSD_CONV_DIM)[:, seq - (SSD_CONV - 1):][None],
        fbuf_p[None],
        heads5(sm["k"], nb, 1),
        heads5(sm["v"], nb, 1),
        lf_s.reshape(1, nb, 1, FOX_HEADS),
        st_s.reshape(1, nb, SSD_HEADS, SSD_HEAD_DIM, SSD_STATE),
        jnp.concatenate([state_ssd_conv[0][:, 1:], sm["xbc"][:, None, :]], axis=1)[None],
        jnp.concatenate([state_ffn_conv[0][:, 1:], g_s[:, None, :]], axis=1)[None],
    )
```

```python
import functools

import jax
import jax.numpy as jnp
from jax import lax
from jax.experimental import pallas as pl
from jax.experimental.pallas import tpu as pltpu

F32 = jnp.float32
BF16 = jnp.bfloat16

D_MODEL = 2048
SSD_INNER = 4096
SSD_HEAD_DIM = 64
SSD_HEADS = 64
SSD_GROUPS = 8
SSD_STATE = 128
SSD_CONV = 4
SSD_CHUNK = 128
SSD_CONV_DIM = SSD_INNER + 2 * SSD_GROUPS * SSD_STATE
HEADS_PER_GROUP = SSD_HEADS // SSD_GROUPS
GROUP_WIDTH = HEADS_PER_GROUP * SSD_HEAD_DIM
FOX_HEAD_DIM = 128
FOX_HEADS = 16
FOX_WIDTH = FOX_HEADS * FOX_HEAD_DIM
MEM_HEADS = 4
MEM_HEAD_DIM = 128
MEM_WIDTH = MEM_HEADS * MEM_HEAD_DIM
D_FF = 5632
FFN_CONV = 3
N_BRANCH = 3
EPS = 1e-6

LANES = 128
SUBLANES = 8
VMEM_LIMIT_BYTES = 56 * 1024 * 1024

DTF_WIDTH = LANES
PAGES_PER_STEP = 8


def _cparams(*semantics):
    return pltpu.CompilerParams(dimension_semantics=semantics, vmem_limit_bytes=VMEM_LIMIT_BYTES)


def _dot(a, b):
    return jnp.dot(a, b, preferred_element_type=F32)


def _dot_nt(a, b):
    return lax.dot_general(a, b, (((1,), (1,)), ((), ())), preferred_element_type=F32)


def _sigmoid(x):
    return 0.5 * jnp.tanh(0.5 * x) + 0.5


def _silu(x):
    return x * _sigmoid(x)


def _softplus(x):
    return jnp.maximum(x, 0.0) + jnp.log1p(jnp.exp(-jnp.abs(x)))


def _split3(x):
    hi = x.astype(BF16)
    r1 = x - hi.astype(F32)
    mid = r1.astype(BF16)
    lo = (r1 - mid.astype(F32)).astype(BF16)
    return hi, mid, lo


def _rms_kernel(x_ref, g_ref, o_ref):
    x = x_ref[...]
    ms = jnp.mean(x * x, axis=-1, keepdims=True)
    o_ref[...] = (x * lax.rsqrt(ms + EPS) * g_ref[...]).astype(o_ref.dtype)


def rms_norm(x, g, *, tm, out_dtype=BF16):
    m, d = x.shape
    return pl.pallas_call(
        _rms_kernel,
        grid=(m // tm,),
        in_specs=[pl.BlockSpec((tm, d), lambda i: (i, 0)), pl.BlockSpec((1, d), lambda i: (0, 0))],
        out_specs=pl.BlockSpec((tm, d), lambda i: (i, 0)),
        out_shape=jax.ShapeDtypeStruct((m, d), out_dtype),
        compiler_params=_cparams("parallel"),
        name="rms_norm",
    )(x, g.reshape(1, d))


def _mm_kernel(*refs, epilogue, n_rows):
    a_ref, w_ref = refs[0], refs[1]
    rows = [r[...] for r in refs[2:2 + n_rows]]
    outs = refs[2 + n_rows:]
    acc = _dot(a_ref[...], w_ref[...].astype(BF16))
    vals = epilogue(acc, *rows)
    for o, v in zip(outs, vals):
        o[...] = v.astype(o.dtype)


def matmul(a, w, *, tm, tn, out_dtypes, epilogue=lambda acc: (acc,), rows=(), name="matmul", cols=None):
    m, k = a.shape
    tm = min(tm, m)
    if cols is None:
        n = w.shape[1]
        tn = min(tn, n)
        w_spec = pl.BlockSpec((k, tn), lambda i, j: (0, j))
    else:
        start, n = cols
        tn = min(tn, n)
        assert start % tn == 0 and n % tn == 0
        first = start // tn
        w_spec = pl.BlockSpec((None, k, tn), lambda i, j: (0, 0, first + j))
    in_specs = [pl.BlockSpec((tm, k), lambda i, j: (i, 0)), w_spec]
    in_specs += [pl.BlockSpec((1, tn), lambda i, j: (0, j)) for _ in rows]
    outs = pl.pallas_call(
        functools.partial(_mm_kernel, epilogue=epilogue, n_rows=len(rows)),
        grid=(m // tm, n // tn),
        in_specs=in_specs,
        out_specs=[pl.BlockSpec((tm, tn), lambda i, j: (i, j)) for _ in out_dtypes],
        out_shape=[jax.ShapeDtypeStruct((m, n), dt) for dt in out_dtypes],
        compiler_params=_cparams("parallel", "arbitrary"),
        name=name,
    )(a, w, *rows)
    return outs


def _dtf_epilogue(acc, bias):
    x = acc + bias
    lane = lax.broadcasted_iota(jnp.int32, x.shape, 1)
    return (jnp.where(lane < SSD_HEADS, _softplus(x), -_softplus(-x)),)


def _cumsum_kernel(x_ref, o_ref):
    x = x_ref[...]
    n = x.shape[0]
    row = lax.broadcasted_iota(jnp.int32, x.shape, 0)
    s = 1
    while s < n:
        x = x + jnp.where(row >= s, pltpu.roll(x, s, 0), 0.0)
        s *= 2
    o_ref[...] = x


def seq_cumsum(x, *, seq):
    m, d = x.shape
    return pl.pallas_call(
        _cumsum_kernel,
        grid=(m // seq,),
        in_specs=[pl.BlockSpec((seq, d), lambda b: (b, 0))],
        out_specs=pl.BlockSpec((seq, d), lambda b: (b, 0)),
        out_shape=jax.ShapeDtypeStruct((m, d), F32),
        compiler_params=_cparams("parallel"),
        name="seq_cumsum",
    )(x)


def _pair_cols(mat, pair, lane_lo):
    a = mat[:, 2 * pair:2 * pair + 1]
    b = mat[:, 2 * pair + 1:2 * pair + 2]
    return jnp.where(lane_lo, a, b)


def _ssd_kernel(xbc_ref, dtf_ref, zs_ref, cw_ref, cb_ref, a_ref, dskip_ref, gn_ref,
                o_ref, st_out_ref, xbuf, xc, st, y_scr, xw_scr):
    c = pl.program_id(1)
    nc = pl.num_programs(1)
    cs = SSD_CHUNK
    halo = SUBLANES

    @pl.when(c == 0)
    def _():
        xbuf[0:halo, :] = jnp.zeros((halo, SSD_CONV_DIM), F32)
        st[...] = jnp.zeros_like(st)

    @pl.when(c > 0)
    def _():
        xbuf[0:halo, :] = xbuf[cs:cs + halo, :]

    xbuf[halo:halo + cs, :] = xbc_ref[...]

    cw = cw_ref[...]
    cb = cb_ref[...]
    col_chunk = 512
    for j in range(SSD_CONV_DIM // col_chunk):
        sl = slice(j * col_chunk, (j + 1) * col_chunk)
        acc = xbuf[halo - 3:halo - 3 + cs, sl] * cw[0:1, sl]
        for i in range(1, SSD_CONV):
            acc = acc + xbuf[halo - 3 + i:halo - 3 + i + cs, sl] * cw[i:i + 1, sl]
        xc[:, sl] = _silu(acc + cb[:, sl])

    dt = dtf_ref[...]
    da = dt * a_ref[...]
    r_i = lax.broadcasted_iota(jnp.int32, (cs, cs), 0)
    c_i = lax.broadcasted_iota(jnp.int32, (cs, cs), 1)
    causal = c_i <= r_i
    tril = jnp.where(causal, 1.0, 0.0)
    a_cs = jnp.dot(tril, da, preferred_element_type=F32, precision=lax.Precision.HIGHEST)
    a_cs_t = a_cs.T
    dt_t = dt.T
    last = a_cs[cs - 1:cs, :]
    ecs = jnp.exp(a_cs)
    wfac = dt * jnp.exp(last - a_cs)
    cdec = jnp.exp(last)

    lane_lo = lax.broadcasted_iota(jnp.int32, (cs, LANES), 1) < SSD_HEAD_DIM
    lane_lo_row = lane_lo[0:1, :]
    pairs_per_group = HEADS_PER_GROUP // 2
    for g in range(SSD_GROUPS):
        b_off = SSD_INNER + g * SSD_STATE
        c_off = SSD_INNER + SSD_GROUPS * SSD_STATE + g * SSD_STATE
        bg = xc[:, b_off:b_off + SSD_STATE]
        cg = xc[:, c_off:c_off + SSD_STATE].astype(BF16)
        cbm = _dot_nt(cg, bg.astype(BF16))
        yoff = _dot(cg, st[g].astype(BF16))
        cd_tiles = []
        for pr in range(pairs_per_group):
            pair = g * pairs_per_group + pr
            col0 = pair * LANES
            xs_pair = xc[:, col0:col0 + LANES]
            xs_bf = xs_pair.astype(BF16)
            yd = []
            for h in (2 * pair, 2 * pair + 1):
                seg = a_cs[:, h:h + 1] - a_cs_t[h:h + 1, :]
                decay = jnp.exp(jnp.where(causal, seg, -jnp.inf))
                mh = (cbm * decay * dt_t[h:h + 1, :]).astype(BF16)
                yd.append(_dot(mh, xs_bf))
            y_pair = jnp.where(lane_lo, yd[0], yd[1])
            y_pair = y_pair + _pair_cols(ecs, pair, lane_lo) * yoff[:, pr * LANES:(pr + 1) * LANES]
            y_pair = y_pair + xs_pair * dskip_ref[:, col0:col0 + LANES]
            y_scr[:, col0:col0 + LANES] = y_pair
            xw_scr[:, pr * LANES:(pr + 1) * LANES] = (xs_pair * _pair_cols(wfac, pair, lane_lo)).astype(BF16)
            cd_tiles.append(_pair_cols(cdec, pair, lane_lo_row))
        cd_row = jnp.concatenate(cd_tiles, axis=1)
        st[g] = st[g] * cd_row + _dot(bg.T.astype(BF16), xw_scr[...])

    v = y_scr[...] * zs_ref[...]
    ms = jnp.mean(v * v, axis=-1, keepdims=True)
    o_ref[...] = (v * lax.rsqrt(ms + EPS) * gn_ref[...]).astype(o_ref.dtype)

    @pl.when(c == nc - 1)
    def _():
        for g in range(SSD_GROUPS):
            st_out_ref[0, g * GROUP_WIDTH:(g + 1) * GROUP_WIDTH, :] = st[g].T


def ssd_prompt(xbc, dtf, zs, conv_w, conv_b, a_row, dskip_row, gnorm, *, batch, seq):
    nc = seq // SSD_CHUNK
    cs = SSD_CHUNK
    row = lambda b, c: (b * nc + c, 0)
    const = lambda b, c: (0, 0)
    return pl.pallas_call(
        _ssd_kernel,
        grid=(batch, nc),
        in_specs=[
            pl.BlockSpec((cs, SSD_CONV_DIM), row),
            pl.BlockSpec((cs, DTF_WIDTH), row),
            pl.BlockSpec((cs, SSD_INNER), row),
            pl.BlockSpec((SSD_CONV, SSD_CONV_DIM), const),
            pl.BlockSpec((1, SSD_CONV_DIM), const),
            pl.BlockSpec((1, DTF_WIDTH), const),
            pl.BlockSpec((1, SSD_INNER), const),
            pl.BlockSpec((1, SSD_INNER), const),
        ],
        out_specs=[
            pl.BlockSpec((cs, SSD_INNER), row),
            pl.BlockSpec((1, SSD_HEADS * SSD_HEAD_DIM, SSD_STATE), lambda b, c: (b, 0, 0)),
        ],
        out_shape=[
            jax.ShapeDtypeStruct((batch * seq, SSD_INNER), BF16),
            jax.ShapeDtypeStruct((batch, SSD_HEADS * SSD_HEAD_DIM, SSD_STATE), F32),
        ],
        scratch_shapes=[
            pltpu.VMEM((cs + SUBLANES, SSD_CONV_DIM), F32),
            pltpu.VMEM((cs, SSD_CONV_DIM), F32),
            pltpu.VMEM((SSD_GROUPS, SSD_STATE, GROUP_WIDTH), F32),
            pltpu.VMEM((cs, SSD_INNER), F32),
            pltpu.VMEM((cs, GROUP_WIDTH), BF16),
        ],
        compiler_params=_cparams("parallel", "arbitrary"),
        name="ssd_prompt",
    )(xbc, dtf, zs, conv_w, conv_b, a_row, dskip_row, gnorm)


def _fox_kernel(q_ref, k_ref, v_ref, cq_ref, ck_ref, o_ref, *, seq, row_blocks, heads, scale):
    dh = FOX_HEAD_DIM
    blk = seq // row_blocks
    for h in range(heads):
        cols = slice(h * dh, (h + 1) * dh)
        for r in range(row_blocks):
            r0, kw = r * blk, (r + 1) * blk
            s = _dot_nt(q_ref[r0:r0 + blk, cols], k_ref[0:kw, cols]) * scale
            s = s + cq_ref[0, h, r0:r0 + blk, :] - ck_ref[0, h, :, 0:kw]
            r_i = lax.broadcasted_iota(jnp.int32, (blk, kw), 0) + r0
            c_i = lax.broadcasted_iota(jnp.int32, (blk, kw), 1)
            s = jnp.where(c_i <= r_i, s, -jnp.inf)
            p = jnp.exp(s - jnp.max(s, axis=-1, keepdims=True))
            l = jnp.sum(p, axis=-1, keepdims=True)
            o_ref[r0:r0 + blk, cols] = (_dot(p.astype(BF16), v_ref[0:kw, cols]) / l).astype(o_ref.dtype)


def fox_prompt(q, k, v, cq, ck, *, batch, seq, row_blocks, heads):
    width = heads * FOX_HEAD_DIM
    tokens = pl.BlockSpec((seq, width), lambda b, h: (b, h))
    return pl.pallas_call(
        functools.partial(_fox_kernel, seq=seq, row_blocks=row_blocks, heads=heads, scale=FOX_HEAD_DIM ** -0.5),
        grid=(batch, FOX_HEADS // heads),
        in_specs=[tokens, tokens, tokens,
                  pl.BlockSpec((1, heads, seq, 1), lambda b, h: (b, h, 0, 0)),
                  pl.BlockSpec((1, heads, 1, seq), lambda b, h: (b, h, 0, 0))],
        out_specs=tokens,
        out_shape=jax.ShapeDtypeStruct((batch * seq, FOX_WIDTH), BF16),
        compiler_params=_cparams("parallel", "arbitrary"),
        name="fox_prompt",
    )(q, k, v, cq, ck)


def _mem_kernel(q_ref, k_ref, v_ref, o_ref, *, scale):
    dh = MEM_HEAD_DIM
    for h in range(MEM_HEADS):
        cols = slice(h * dh, (h + 1) * dh)
        s = _dot_nt(q_ref[:, cols], k_ref[:, cols]) * scale
        e = jnp.exp(s - jnp.max(s, axis=-1, keepdims=True))
        p = e / jnp.sum(e, axis=-1, keepdims=True)
        o_ref[:, cols] = _dot(p.astype(BF16), v_ref[:, cols]).astype(o_ref.dtype)


def mem_prompt_attend(q, k, v, *, batch, seq, n_mem, tq):
    nq = seq // tq
    return pl.pallas_call(
        functools.partial(_mem_kernel, scale=MEM_HEAD_DIM ** -0.5),
        grid=(batch, nq),
        in_specs=[
            pl.BlockSpec((tq, MEM_WIDTH), lambda b, i: (b * nq + i, 0)),
            pl.BlockSpec((n_mem, MEM_WIDTH), lambda b, i: (b, 0)),
            pl.BlockSpec((n_mem, MEM_WIDTH), lambda b, i: (b, 0)),
        ],
        out_specs=pl.BlockSpec((tq, MEM_WIDTH), lambda b, i: (b * nq + i, 0)),
        out_shape=jax.ShapeDtypeStruct((batch * seq, MEM_WIDTH), BF16),
        compiler_params=_cparams("parallel", "arbitrary"),
        name="mem_prompt_attend",
    )(q, k, v)


def _merge_kernel(os_ref, of_ref, om_ref, ws_ref, wf_ref, wm_ref, g0_ref, g1_ref, g2_ref, o_ref):
    u = g0_ref[...] * _dot(os_ref[...], ws_ref[...].astype(BF16))
    u = u + g1_ref[...] * _dot(of_ref[...], wf_ref[...].astype(BF16))
    u = u + g2_ref[...] * _dot(om_ref[...], wm_ref[...].astype(BF16))
    o_ref[...] = u.astype(o_ref.dtype)


def branch_merge(o_ssd, o_fox, o_mem, w_branch, gates, *, tm, tn):
    m = o_ssd.shape[0]
    tm = min(tm, m)
    nj = D_MODEL // tn
    act = lambda k: pl.BlockSpec((tm, k), lambda i, j: (i, 0))
    gate = lambda br: pl.BlockSpec((tm, tn), lambda i, j: (i, br * nj + j))

    def wgt(row0, k):
        assert row0 % k == 0
        return pl.BlockSpec((None, k, tn), lambda i, j: (0, row0 // k, j))

    return pl.pallas_call(
        _merge_kernel,
        grid=(m // tm, nj),
        in_specs=[act(SSD_INNER), act(FOX_WIDTH), act(MEM_WIDTH),
                  wgt(0, SSD_INNER), wgt(SSD_INNER, FOX_WIDTH), wgt(SSD_INNER + FOX_WIDTH, MEM_WIDTH),
                  gate(0), gate(1), gate(2)],
        out_specs=pl.BlockSpec((tm, tn), lambda i, j: (i, j)),
        out_shape=jax.ShapeDtypeStruct((m, D_MODEL), BF16),
        compiler_params=_cparams("parallel", "arbitrary"),
        name="branch_merge",
    )(o_ssd, o_fox, o_mem, w_branch, w_branch, w_branch, gates, gates, gates)


def _outproj_kernel(a_ref, w_ref, x_ref, g_ref, h_ref, hn_ref):
    h = x_ref[...] + _dot(a_ref[...], w_ref[...])
    h_ref[...] = h
    ms = jnp.mean(h * h, axis=-1, keepdims=True)
    hn_ref[...] = (h * lax.rsqrt(ms + EPS) * g_ref[...]).astype(hn_ref.dtype)


def out_proj(merged, w_out, x, g_ffn, *, tm):
    m = x.shape[0]
    tm = min(tm, m)
    row = lambda i: (i, 0)
    const = lambda i: (0, 0)
    return pl.pallas_call(
        _outproj_kernel,
        grid=(m // tm,),
        in_specs=[pl.BlockSpec((tm, D_MODEL), row), pl.BlockSpec((D_MODEL, D_MODEL), const),
                  pl.BlockSpec((tm, D_MODEL), row), pl.BlockSpec((1, D_MODEL), const)],
        out_specs=[pl.BlockSpec((tm, D_MODEL), row), pl.BlockSpec((tm, D_MODEL), row)],
        out_shape=[jax.ShapeDtypeStruct((m, D_MODEL), F32), jax.ShapeDtypeStruct((m, D_MODEL), BF16)],
        compiler_params=_cparams("parallel"),
        name="out_proj",
    )(merged, w_out, x, g_ffn.reshape(1, D_MODEL))


def _ffn_kernel(hn_ref, wg_ref, wu_ref, cw_ref, cb_ref, act_ref, buf_ref, gbuf, *, tiles_per_seq):
    i = pl.program_id(1)
    tm = hn_ref.shape[0]
    halo = SUBLANES

    @pl.when(i % tiles_per_seq == 0)
    def _():
        gbuf[0:halo, :] = jnp.zeros((halo, gbuf.shape[1]), F32)

    @pl.when(i % tiles_per_seq != 0)
    def _():
        gbuf[0:halo, :] = gbuf[tm:tm + halo, :]

    hn = hn_ref[...]
    gbuf[halo:halo + tm, :] = _dot(hn, wg_ref[...].astype(BF16))
    cw = cw_ref[...]
    conv = gbuf[halo - 2:halo - 2 + tm, :] * cw[0:1, :]
    conv = conv + gbuf[halo - 1:halo - 1 + tm, :] * cw[1:2, :]
    conv = conv + gbuf[halo:halo + tm, :] * cw[2:3, :]
    conv = conv + cb_ref[...]
    act_ref[...] = (_silu(conv) * _dot(hn, wu_ref[...].astype(BF16))).astype(act_ref.dtype)
    buf_ref[0] = gbuf[halo + tm - 2:halo + tm, :]


def ffn_gate_up(hn, w_gate, w_up, conv_w, conv_b, *, batch, seq, tm, tn):
    m = hn.shape[0]
    tiles_per_seq = seq // tm
    return pl.pallas_call(
        functools.partial(_ffn_kernel, tiles_per_seq=tiles_per_seq),
        grid=(D_FF // tn, m // tm),
        in_specs=[
            pl.BlockSpec((tm, D_MODEL), lambda j, i: (i, 0)),
            pl.BlockSpec((None, D_MODEL, tn), lambda j, i: (0, 0, j)),
            pl.BlockSpec((None, D_MODEL, tn), lambda j, i: (0, 0, j)),
            pl.BlockSpec((FFN_CONV, tn), lambda j, i: (0, j)),
            pl.BlockSpec((1, tn), lambda j, i: (0, j)),
        ],
        out_specs=[
            pl.BlockSpec((tm, tn), lambda j, i: (i, j)),
            pl.BlockSpec((1, FFN_CONV - 1, tn), lambda j, i: (i // tiles_per_seq, 0, j)),
        ],
        out_shape=[
            jax.ShapeDtypeStruct((m, D_FF), BF16),
            jax.ShapeDtypeStruct((batch, FFN_CONV - 1, D_FF), F32),
        ],
        scratch_shapes=[pltpu.VMEM((tm + SUBLANES, tn), F32)],
        compiler_params=_cparams("parallel", "arbitrary"),
        name="ffn_gate_up",
    )(hn, w_gate, w_up, conv_w, conv_b)


def _down_kernel(a_ref, w_ref, h_ref, g_ref, y_ref, acc):
    k = pl.program_id(1)

    @pl.when(k == 0)
    def _():
        acc[...] = h_ref[...]

    acc[...] += _dot(a_ref[...], w_ref[...])

    @pl.when(k == pl.num_programs(1) - 1)
    def _():
        h = acc[...]
        ms = jnp.mean(h * h, axis=-1, keepdims=True)
        y_ref[...] = h * lax.rsqrt(ms + EPS) * g_ref[...]


def ffn_down(act, w_down, h, g_final, *, tm, tk):
    m = h.shape[0]
    tm = min(tm, m)
    return pl.pallas_call(
        _down_kernel,
        grid=(m // tm, D_FF // tk),
        in_specs=[pl.BlockSpec((tm, tk), lambda i, k: (i, k)), pl.BlockSpec((tk, D_MODEL), lambda i, k: (k, 0)),
                  pl.BlockSpec((tm, D_MODEL), lambda i, k: (i, 0)), pl.BlockSpec((1, D_MODEL), lambda i, k: (0, 0))],
        out_specs=pl.BlockSpec((tm, D_MODEL), lambda i, k: (i, 0)),
        out_shape=jax.ShapeDtypeStruct((m, D_MODEL), F32),
        scratch_shapes=[pltpu.VMEM((tm, D_MODEL), F32)],
        compiler_params=_cparams("parallel", "arbitrary"),
        name="ffn_down",
    )(act, w_down, h, g_final.reshape(1, D_MODEL))


def _ssd_s_prep_kernel(xbc_ref, buf_ref, cw_ref, cb_ref, dtf_ref, a_ref, expand_ref, xc_ref, cols_ref):
    cw = cw_ref[...]
    acc = buf_ref[0] * cw[0:1, :]
    acc = acc + buf_ref[1] * cw[1:2, :]
    acc = acc + buf_ref[2] * cw[2:3, :]
    acc = acc + xbc_ref[...] * cw[3:4, :]
    xcv = _silu(acc + cb_ref[...])
    xc_ref[...] = xcv
    dt = dtf_ref[...]
    decay = jnp.exp(dt * a_ref[...])
    nb = dt.shape[0]
    parts = _split3(dt) + _split3(decay)
    stacked = jnp.concatenate(parts, axis=0)
    ex = _dot(stacked, expand_ref[...])
    dt_exp = ex[0:nb] + ex[nb:2 * nb] + ex[2 * nb:3 * nb]
    dec_exp = ex[3 * nb:4 * nb] + ex[4 * nb:5 * nb] + ex[5 * nb:6 * nb]
    xdt = xcv[:, 0:SSD_INNER] * dt_exp
    pad = jnp.zeros((LANES - 2 * nb, SSD_INNER), F32)
    cols_ref[...] = jnp.concatenate([xdt, dec_exp, pad], axis=0).T


def ssd_sample_prep(xbc, buf_t, conv_w, conv_b, dtf, a_row, expand):
    nb = xbc.shape[0]
    full = lambda s: pl.BlockSpec(s, lambda i: (0,) * len(s))
    return pl.pallas_call(
        _ssd_s_prep_kernel,
        grid=(1,),
        in_specs=[full(xbc.shape), full(buf_t.shape), full(conv_w.shape), full(conv_b.shape), full(dtf.shape),
                  full(a_row.shape), full(expand.shape)],
        out_specs=[full((nb, SSD_CONV_DIM)), full((SSD_INNER, LANES))],
        out_shape=[jax.ShapeDtypeStruct((nb, SSD_CONV_DIM), F32), jax.ShapeDtypeStruct((SSD_INNER, LANES), F32)],
        compiler_params=_cparams("arbitrary"),
        name="ssd_sample_prep",
    )(xbc, buf_t, conv_w, conv_b, dtf, a_row, expand)


def _ssd_s_step_kernel(cols_ref, h0_ref, xc_ref, zs_ref, dskip_ref, gn_ref, hn_ref, o_ref, *, nb):
    b = pl.program_id(0)
    cols = cols_ref[...]
    hi, mid, lo = _split3(cols)
    lhs = jnp.concatenate([hi, mid, lo], axis=1)
    r_i = lax.broadcasted_iota(jnp.int32, (3 * LANES, 2 * LANES), 0) % LANES
    c_i = lax.broadcasted_iota(jnp.int32, (3 * LANES, 2 * LANES), 1)
    sel = jnp.where(r_i == jnp.where(c_i < LANES, b, nb + b), 1.0, 0.0).astype(BF16)
    picked = _dot(lhs, sel)
    xb = picked[:, 0:LANES]
    db = picked[:, LANES:2 * LANES]
    xc = xc_ref[0]
    y_parts = []
    for g in range(SSD_GROUPS):
        rows = slice(g * GROUP_WIDTH, (g + 1) * GROUP_WIDTH)
        b_row = xc[:, SSD_INNER + g * SSD_STATE:SSD_INNER + (g + 1) * SSD_STATE]
        c_off = SSD_INNER + SSD_GROUPS * SSD_STATE + g * SSD_STATE
        c_row = xc[:, c_off:c_off + SSD_STATE]
        hn = h0_ref[0, rows, :] * db[rows, :] + xb[rows, :] * b_row
        hn_ref[0, rows, :] = hn
        c8 = jnp.broadcast_to(c_row, (SUBLANES, SSD_STATE)).astype(BF16)
        y_parts.append(_dot_nt(c8, hn.astype(BF16))[0:1, :])
    xs = xc[:, 0:SSD_INNER]
    y = jnp.concatenate(y_parts, axis=1) + xs * dskip_ref[...]
    v = y * zs_ref[0]
    ms = jnp.mean(v * v, axis=-1, keepdims=True)
    o_ref[0] = (v * lax.rsqrt(ms + EPS) * gn_ref[...]).astype(o_ref.dtype)


def ssd_sample_step(cols, h0, xc, zs, dskip_row, gnorm):
    nb = h0.shape[0]
    rows = SSD_HEADS * SSD_HEAD_DIM
    const2 = lambda b: (0, 0)
    per_b = lambda b: (b, 0, 0)
    return pl.pallas_call(
        functools.partial(_ssd_s_step_kernel, nb=nb),
        grid=(nb,),
        in_specs=[
            pl.BlockSpec((SSD_INNER, LANES), const2),
            pl.BlockSpec((1, rows, SSD_STATE), per_b),
            pl.BlockSpec((1, 1, SSD_CONV_DIM), per_b),
            pl.BlockSpec((1, 1, SSD_INNER), per_b),
            pl.BlockSpec((1, SSD_INNER), const2),
            pl.BlockSpec((1, SSD_INNER), const2),
        ],
        out_specs=[pl.BlockSpec((1, rows, SSD_STATE), per_b), pl.BlockSpec((1, 1, SSD_INNER), per_b)],
        out_shape=[jax.ShapeDtypeStruct((nb, rows, SSD_STATE), F32), jax.ShapeDtypeStruct((nb, 1, SSD_INNER), BF16)],
        compiler_params=_cparams("parallel"),
        name="ssd_sample_step",
    )(cols, h0, xc.reshape(nb, 1, SSD_CONV_DIM), zs.reshape(nb, 1, SSD_INNER), dskip_row, gnorm)


def _head_matched(shape, n_heads):
    r_i = lax.broadcasted_iota(jnp.int32, shape, 0)
    c_i = lax.broadcasted_iota(jnp.int32, shape, 1)
    return c_i % n_heads == r_i % n_heads


def _fox_s_kernel(pt_ref, *refs, n_steps, scale):
    del pt_ref
    pp = PAGES_PER_STEP
    nh = FOX_HEADS
    q_ref, kn_ref, vn_ref, lfn_ref, suf_ref = refs[0:5]
    k_refs = refs[5:5 + pp]
    v_refs = refs[5 + pp:5 + 2 * pp]
    lf_refs = refs[5 + 2 * pp:5 + 3 * pp]
    o_ref = refs[5 + 3 * pp]
    m_scr, l_scr, acc_scr, carry = refs[6 + 3 * pp:]
    j = pl.program_id(1)

    @pl.when(j == 0)
    def _():
        m_scr[...] = jnp.full_like(m_scr, -jnp.inf)
        l_scr[...] = jnp.zeros_like(l_scr)
        acc_scr[...] = jnp.zeros_like(acc_scr)
        carry[...] = lfn_ref[0]

    q = q_ref[0].astype(BF16)
    page = k_refs[0].shape[2]
    rows = page * nh // LANES
    keep = _head_matched((nh, page * nh), nh)
    row_i = lax.broadcasted_iota(jnp.int32, (rows, LANES), 0)

    run = carry[...]
    scores = []
    for i in range(pp):
        lfd = lf_refs[i][0]
        r = _dot(jnp.concatenate(_split3(lfd), axis=0), suf_ref[...])
        within = r[0:rows, 0:LANES] + r[rows:2 * rows, 0:LANES] + r[2 * rows:3 * rows, 0:LANES]
        rowtot = r[0:rows, LANES:] + r[rows:2 * rows, LANES:] + r[2 * rows:3 * rows, LANES:]
        incl = rowtot
        sft = 1
        while sft < rows:
            incl = incl + jnp.where(row_i + sft < rows, pltpu.roll(incl, rows - sft, 0), 0.0)
            sft *= 2
        bias_d = within + (incl - rowtot) + run
        run = run + incl[0:1, :]
        bias = jnp.concatenate([jnp.broadcast_to(bias_d[t:t + 1, :], (nh, LANES)) for t in range(rows)], axis=1)
        k2 = k_refs[i][0, 0].reshape(page * nh, FOX_HEAD_DIM).astype(BF16)
        scores.append(jnp.where(keep, _dot_nt(q, k2) * scale + bias, -jnp.inf))
    carry[...] = run

    m_old = m_scr[...]
    m_new = m_old
    for s in scores:
        m_new = jnp.maximum(m_new, jnp.max(s, axis=-1, keepdims=True))
    alpha = jnp.exp(m_old - m_new)
    l = alpha * l_scr[...]
    acc = alpha * acc_scr[...]
    for i, s in enumerate(scores):
        p = jnp.exp(s - m_new)
        l = l + jnp.sum(p, axis=-1, keepdims=True)
        v2 = v_refs[i][0, 0].reshape(page * nh, FOX_HEAD_DIM).astype(BF16)
        acc = acc + _dot(p.astype(BF16), v2)
    m_scr[...] = m_new
    l_scr[...] = l
    acc_scr[...] = acc

    @pl.when(j == n_steps - 1)
    def _():
        s_self = jnp.sum(q_ref[0] * kn_ref[0], axis=-1, keepdims=True) * scale
        m_old = m_scr[...]
        m_new = jnp.maximum(m_old, s_self)
        alpha = jnp.exp(m_old - m_new)
        p_self = jnp.exp(s_self - m_new)
        l = alpha * l_scr[...] + p_self
        acc = alpha * acc_scr[...] + p_self * vn_ref[0]
        o_ref[0] = (acc / l).astype(o_ref.dtype)


def fox_sample(page_table, q, k_new, v_new, lf_new_tiled, k_pool, v_pool, lf_pool_dense):
    nb, n_pages = page_table.shape
    page = k_pool.shape[2]
    rows = lf_pool_dense.shape[1]
    pp = PAGES_PER_STEP
    n_steps = n_pages // pp
    r_i = lax.broadcasted_iota(jnp.int32, (LANES, 2 * LANES), 0)
    c_i = lax.broadcasted_iota(jnp.int32, (LANES, 2 * LANES), 1)
    same = r_i % FOX_HEADS == c_i % FOX_HEADS
    suf = (same & ((c_i >= LANES) | (r_i > c_i))).astype(BF16)

    def newest_first(i):
        return lambda b, j, pt: pt[b, n_pages - 1 - (j * pp + i)]

    def kv_spec(i):
        pick = newest_first(i)
        return pl.BlockSpec((1, 1, page, FOX_HEADS, FOX_HEAD_DIM), lambda b, j, pt: (0, pick(b, j, pt), 0, 0, 0))

    def lf_spec(i):
        pick = newest_first(i)
        return pl.BlockSpec((1, rows, LANES), lambda b, j, pt: (pick(b, j, pt), 0, 0))

    per_b = lambda shape: pl.BlockSpec((1,) + shape, lambda b, j, pt: (b, 0, 0))
    hd = (FOX_HEADS, FOX_HEAD_DIM)
    in_specs = [per_b(hd), per_b(hd), per_b(hd), per_b((1, LANES)),
                pl.BlockSpec((LANES, 2 * LANES), lambda b, j, pt: (0, 0))]
    in_specs += [kv_spec(i) for i in range(pp)]
    in_specs += [kv_spec(i) for i in range(pp)]
    in_specs += [lf_spec(i) for i in range(pp)]
    grid_spec = pltpu.PrefetchScalarGridSpec(
        num_scalar_prefetch=1,
        grid=(nb, n_steps),
        in_specs=in_specs,
        out_specs=per_b(hd),
        scratch_shapes=[
            pltpu.VMEM((FOX_HEADS, 1), F32),
            pltpu.VMEM((FOX_HEADS, 1), F32),
            pltpu.VMEM(hd, F32),
            pltpu.VMEM((1, LANES), F32),
        ],
    )
    return pl.pallas_call(
        functools.partial(_fox_s_kernel, n_steps=n_steps, scale=FOX_HEAD_DIM ** -0.5),
        grid_spec=grid_spec,
        out_shape=jax.ShapeDtypeStruct((nb,) + hd, BF16),
        compiler_params=_cparams("parallel", "arbitrary"),
        name="fox_sample",
    )(page_table, q.reshape((nb,) + hd), k_new.reshape((nb,) + hd), v_new.reshape((nb,) + hd),
      lf_new_tiled.reshape(nb, 1, LANES), suf,
      *([k_pool] * pp), *([v_pool] * pp), *([lf_pool_dense] * pp))


def _mem_s_kernel(q_ref, k_ref, v_ref, o_ref, *, scale):
    n_mem = k_ref.shape[2]
    k2 = k_ref[0, 0].reshape(n_mem * MEM_HEADS, MEM_HEAD_DIM).astype(BF16)
    v2 = v_ref[0, 0].reshape(n_mem * MEM_HEADS, MEM_HEAD_DIM).astype(BF16)
    q = q_ref[0]
    keep = _head_matched((q.shape[0], n_mem * MEM_HEADS), MEM_HEADS)
    s = jnp.where(keep, _dot_nt(q, k2) * scale, -jnp.inf)
    e = jnp.exp(s - jnp.max(s, axis=-1, keepdims=True))
    p = e / jnp.sum(e, axis=-1, keepdims=True)
    o_ref[0] = _dot(p.astype(BF16), v2).astype(o_ref.dtype)


def mem_sample_attend(q_rep, k, v):
    _, nb, n_mem, _, _ = k.shape
    rep = q_rep.shape[1]
    kv = pl.BlockSpec((1, 1, n_mem, MEM_HEADS, MEM_HEAD_DIM), lambda b: (0, b, 0, 0, 0))
    return pl.pallas_call(
        functools.partial(_mem_s_kernel, scale=MEM_HEAD_DIM ** -0.5),
        grid=(nb,),
        in_specs=[pl.BlockSpec((1, rep, MEM_HEAD_DIM), lambda b: (b, 0, 0)), kv, kv],
        out_specs=pl.BlockSpec((1, rep, MEM_HEAD_DIM), lambda b: (b, 0, 0)),
        out_shape=jax.ShapeDtypeStruct((nb, rep, MEM_HEAD_DIM), BF16),
        compiler_params=_cparams("parallel"),
        name="mem_sample_attend",
    )(q_rep, k, v)


def _ffn_s_kernel(g_ref, u_ref, buf_ref, cw_ref, cb_ref, o_ref):
    cw = cw_ref[...]
    conv = buf_ref[0] * cw[0:1, :]
    conv = conv + buf_ref[1] * cw[1:2, :]
    conv = conv + g_ref[...] * cw[2:3, :]
    conv = conv + cb_ref[...]
    o_ref[...] = (_silu(conv) * u_ref[...]).astype(o_ref.dtype)


def ffn_sample_act(g, u, buf_t, conv_w, conv_b):
    full = lambda s: pl.BlockSpec(s, lambda i: (0,) * len(s))
    return pl.pallas_call(
        _ffn_s_kernel,
        grid=(1,),
        in_specs=[full(g.shape), full(u.shape), full(buf_t.shape), full(conv_w.shape), full(conv_b.shape)],
        out_specs=full(g.shape),
        out_shape=jax.ShapeDtypeStruct(g.shape, BF16),
        compiler_params=_cparams("arbitrary"),
        name="ffn_sample_act",
    )(g, u, buf_t, conv_w, conv_b)


def _project(xn, w, *, tm, tn):
    mm = functools.partial(matmul, xn, tm=tm)
    (zs,) = mm(w["z"], tn=tn, out_dtypes=(F32,), epilogue=lambda acc: (_silu(acc),), name="proj_z")
    (xbc,) = mm(w["xbc"], tn=tn, out_dtypes=(F32,), name="proj_xbc")
    (dtf,) = matmul(xn, w["dtf"], tm=min(tm, 1024), tn=DTF_WIDTH, out_dtypes=(F32,), epilogue=_dtf_epilogue,
                    rows=(w["dtf_bias"],), name="proj_dtf")
    (q,) = mm(w["q"], tn=tn, out_dtypes=(BF16,), name="proj_q")
    k, k_bf = mm(w["k"], tn=tn, out_dtypes=(F32, BF16), epilogue=lambda acc: (acc, acc), name="proj_k")
    v, v_bf = mm(w["v"], tn=tn, out_dtypes=(F32, BF16), epilogue=lambda acc: (acc, acc), name="proj_v")
    (mq,) = mm(w["mq"], tn=tn, out_dtypes=(BF16,), name="proj_mq")
    (gates,) = mm(w["gates"], tn=tn, out_dtypes=(F32,), epilogue=lambda acc: (_sigmoid(acc),), name="proj_gates")
    return dict(zs=zs, xbc=xbc, dtf=dtf, q=q, k=k, k_bf=k_bf, v=v, v_bf=v_bf, mq=mq, gates=gates)


def kernel(x_prompt, x_sample, cache_fox_k, cache_fox_v, cache_fox_logf, cache_mem_k, cache_mem_v, state_ssd,
           state_ssd_conv, state_ffn_conv, page_table, mem_prompt, g_mix, w_in, ssd_conv_w, ssd_conv_b,
           ssd_dt_bias, ssd_a_log, ssd_d, ssd_g_norm, fox_b_forget, g_mem, w_mem_kv, w_branch, w_out, g_ffn,
           w_ffn_gate, w_ffn_up, ffn_conv_w, ffn_conv_b, w_ffn_down, g_final):
    depth = w_in.shape[0]
    assert depth == 1, "single-layer trunk"
    bp, seq, d = x_prompt.shape
    nb, dec_seq, _ = x_sample.shape
    assert d == D_MODEL and dec_seq == 1 and seq % SSD_CHUNK == 0
    n_mem = mem_prompt.shape[1]
    n_pool, page = cache_fox_k.shape[1], cache_fox_k.shape[2]
    assert page_table.shape[1] % PAGES_PER_STEP == 0
    mp = bp * seq

    wi = w_in[0]
    offs = [0]
    for width in (SSD_INNER, SSD_CONV_DIM, SSD_HEADS, FOX_WIDTH, FOX_WIDTH, FOX_WIDTH, FOX_HEADS, MEM_WIDTH,
                  N_BRANCH * D_MODEL):
        offs.append(offs[-1] + width)
    assert offs[-1] == wi.shape[1]
    seg = lambda i: wi[:, offs[i]:offs[i + 1]].astype(BF16)
    dtf_pad = DTF_WIDTH - SSD_HEADS - FOX_HEADS
    w = dict(
        z=seg(0), xbc=seg(1), q=seg(3), k=seg(4), v=seg(5), mq=seg(7), gates=seg(8),
        dtf=jnp.concatenate([seg(2), seg(6), jnp.zeros((D_MODEL, dtf_pad), BF16)], axis=1),
        dtf_bias=jnp.concatenate([ssd_dt_bias[0], fox_b_forget[0], jnp.zeros((dtf_pad,), F32)]).reshape(1, DTF_WIDTH),
    )
    w_o = w_out[0].astype(BF16)
    w_d = w_ffn_down[0].astype(BF16)
    w_mk = w_mem_kv[0][:, :MEM_WIDTH].astype(BF16)
    w_mv = w_mem_kv[0][:, MEM_WIDTH:].astype(BF16)
    a_row = jnp.concatenate([-jnp.exp(ssd_a_log[0]), jnp.zeros((DTF_WIDTH - SSD_HEADS,), F32)]).reshape(1, DTF_WIDTH)
    dskip_row = jnp.repeat(ssd_d[0], SSD_HEAD_DIM).reshape(1, SSD_INNER)
    gnorm_row = ssd_g_norm[0].reshape(1, SSD_INNER)
    conv_w = ssd_conv_w[0]
    conv_b = ssd_conv_b[0].reshape(1, SSD_CONV_DIM)
    fconv_w = ffn_conv_w[0]
    fconv_b = ffn_conv_b[0].reshape(1, D_FF)

    xp = x_prompt.reshape(mp, D_MODEL)
    pr = _project(rms_norm(xp, g_mix[0], tm=512), w, tm=2048, tn=512)

    mem_n = rms_norm(mem_prompt.reshape(bp * n_mem, D_MODEL), g_mem[0], tm=512)
    mk_p, mk_bf = matmul(mem_n, w_mk, tm=1024, tn=512, out_dtypes=(F32, BF16), epilogue=lambda acc: (acc, acc),
                         name="proj_mem_k")
    mv_p, mv_bf = matmul(mem_n, w_mv, tm=1024, tn=512, out_dtypes=(F32, BF16), epilogue=lambda acc: (acc, acc),
                         name="proj_mem_v")

    o_ssd_p, st_p = ssd_prompt(pr["xbc"], pr["dtf"], pr["zs"], conv_w, conv_b, a_row, dskip_row, gnorm_row,
                               batch=bp, seq=seq)

    c_all = seq_cumsum(pr["dtf"], seq=seq)
    c_p = c_all[:, SSD_HEADS:SSD_HEADS + FOX_HEADS].reshape(bp, seq, FOX_HEADS).transpose(0, 2, 1)
    o_fox_p = fox_prompt(pr["q"], pr["k_bf"], pr["v_bf"], c_p.reshape(bp, FOX_HEADS, seq, 1),
                         c_p.reshape(bp, FOX_HEADS, 1, seq), batch=bp, seq=seq, row_blocks=4, heads=4)
    o_mem_p = mem_prompt_attend(pr["mq"], mk_bf, mv_bf, batch=bp, seq=seq, n_mem=n_mem, tq=1024)

    merged_p = branch_merge(o_ssd_p, o_fox_p, o_mem_p, w_branch, pr["gates"], tm=1024, tn=256)
    h_p, hn_p = out_proj(merged_p, w_o, xp, g_ffn[0], tm=512)
    act_p, fbuf_p = ffn_gate_up(hn_p, w_ffn_gate, w_ffn_up, fconv_w, fconv_b, batch=bp, seq=seq, tm=1024, tn=512)
    y_p = ffn_down(act_p, w_d, h_p, g_final, tm=512, tk=D_FF // 2)

    xs_ = x_sample.reshape(nb, D_MODEL)
    sm = _project(rms_norm(xs_, g_mix[0], tm=nb), w, tm=nb, tn=2048)

    buf_t = state_ssd_conv[0].transpose(1, 0, 2)
    h_i = lax.broadcasted_iota(jnp.int32, (DTF_WIDTH, SSD_INNER), 0)
    col_i = lax.broadcasted_iota(jnp.int32, (DTF_WIDTH, SSD_INNER), 1)
    expand = (col_i // SSD_HEAD_DIM == h_i).astype(BF16)
    xc_s, cols_s = ssd_sample_prep(sm["xbc"], buf_t, conv_w, conv_b, sm["dtf"], a_row, expand)
    st_s, o_ssd_s = ssd_sample_step(cols_s, state_ssd[0].reshape(nb, SSD_HEADS * SSD_HEAD_DIM, SSD_STATE), xc_s,
                                    sm["zs"], dskip_row, gnorm_row)

    lf_s = sm["dtf"][:, SSD_HEADS:SSD_HEADS + FOX_HEADS]
    lf_dense = cache_fox_logf[0].reshape(n_pool, page * FOX_HEADS // LANES, LANES)
    o_fox_s = fox_sample(page_table, sm["q"].astype(F32), sm["k"], sm["v"], jnp.tile(lf_s, (1, LANES // FOX_HEADS)),
                         cache_fox_k, cache_fox_v, lf_dense)
    mq_rep = jnp.tile(sm["mq"].reshape(nb, MEM_HEADS, MEM_HEAD_DIM), (1, 2 * SUBLANES // MEM_HEADS, 1))
    o_mem_s = mem_sample_attend(mq_rep, cache_mem_k, cache_mem_v)
    o_mem_s = o_mem_s[:, :MEM_HEADS].reshape(nb, MEM_WIDTH)

    merged_s = branch_merge(o_ssd_s.reshape(nb, SSD_INNER), o_fox_s.reshape(nb, FOX_WIDTH), o_mem_s, w_branch,
                            sm["gates"], tm=nb, tn=512)
    h_s, hn_s = out_proj(merged_s, w_o, xs_, g_ffn[0], tm=nb)
    (g_s,) = matmul(hn_s, w_ffn_gate, cols=(0, D_FF), tm=nb, tn=D_FF // 4, out_dtypes=(F32,), name="ffn_gate_s")
    (u_s,) = matmul(hn_s, w_ffn_up, cols=(0, D_FF), tm=nb, tn=D_FF // 4, out_dtypes=(F32,), name="ffn_up_s")
    act_s = ffn_sample_act(g_s, u_s, state_ffn_conv[0].transpose(1, 0, 2), fconv_w, fconv_b)
    y_s = ffn_down(act_s, w_d, h_s, g_final, tm=nb, tk=D_FF // 2)

    lf_p = pr["dtf"][:, SSD_HEADS:SSD_HEADS + FOX_HEADS]
    heads5 = lambda t, b, l: t.reshape(1, b, l, FOX_HEADS, FOX_HEAD_DIM)
    return (
        y_p.reshape(bp, seq, D_MODEL),
        y_s.reshape(nb, 1, D_MODEL),
        heads5(pr["k"], bp, seq),
        heads5(pr["v"], bp, seq),
        lf_p.reshape(1, bp, seq, FOX_HEADS),
        mk_p.reshape(1, bp, n_mem, MEM_HEADS, MEM_HEAD_DIM),
        mv_p.reshape(1, bp, n_mem, MEM_HEADS, MEM_HEAD_DIM),
        st_p.reshape(1, bp, SSD_HEADS, SSD_HEAD_DIM, SSD_STATE),
        pr["xbc"].reshape(bp, seq, SSD_CONV_DIM)[:, seq - (SSD_CONV - 1):][None],
        fbuf_p[None],
        heads5(sm["k"], nb, 1),
        heads5(sm["v"], nb, 1),
        lf_s.reshape(1, nb, 1, FOX_HEADS),
        st_s.reshape(1, nb, SSD_HEADS, SSD_HEAD_DIM, SSD_STATE),
        jnp.concatenate([state_ssd_conv[0][:, 1:], sm["xbc"][:, None, :]], axis=1)[None],
        jnp.concatenate([state_ffn_conv[0][:, 1:], g_s[:, None, :]], axis=1)[None],
    )
```
